```python
import math
import jax, jax.numpy as jnp
from jax import lax
import numpy as np

D_MODEL = 1024
BATCH = 2
SEQ = 16384
DEPTH = 2
DEC_BATCH = 4
DEC_SEQ = 4096
PAST_LEN = 128

N_BRANCH = 3
BRANCH_WIDTH = 512
NORM_EPS = 1e-6
H_A = 4
DK_A = 128
DV_A = 128
CONV_W = 5
DELTA_CHUNK = 64
QKV_A_COLS = 3 * H_A * DK_A
SG_GROUPS = 4
SG_CHUNK = 128
SG_WIDTH = 512
H_C = 4
Q_LORA = 384
KV_LORA = 256
NOPE_DIM = 128
ROPE_DIM = 64
DV_C = 128
QK_HEAD_DIM = NOPE_DIM + ROPE_DIM
ROPE_THETA = 10000.0
SM_SCALE = QK_HEAD_DIM ** -0.5
Q_BLOCK = 128
PEER_HEADS = 8
N_KEYS = 128
N_EXPERTS = N_KEYS * N_KEYS
PEER_QDIM = 256
PEER_TOPK = 16
PEER_BLOCK = 128
IN_SPLIT_SIZES = (QKV_A_COLS, 2 * H_A, 2 * H_A, H_A * DV_A, SG_WIDTH, SG_WIDTH, Q_LORA, KV_LORA, ROPE_DIM, N_BRANCH * D_MODEL)
IN_COLS = sum(IN_SPLIT_SIZES)

kernel_name = 'hybrid_gdn_sgmlp_mla_peer_encoder'


def rms_norm(x, gain):
    xf = x.astype(jnp.float32)
    y = xf * lax.rsqrt(jnp.mean(xf * xf, axis=-1, keepdims=True) + NORM_EPS)
    return (y * gain.astype(jnp.float32)).astype(x.dtype)


def l2_normalize(x):
    return x * lax.rsqrt(jnp.sum(x * x, axis=-1, keepdims=True) + NORM_EPS)


def centred_depthwise_conv(x, w):
    return lax.conv_general_dilated(x, w[:, None, :], window_strides=(1,), padding=[(CONV_W // 2, CONV_W // 2)],
                                    dimension_numbers=('NWC', 'WIO', 'NWC'), feature_group_count=x.shape[-1])


def chunked_delta_rule(q, k, v, log_alpha, beta):
    B, S, H, DK = q.shape
    DV = v.shape[-1]
    C = DELTA_CHUNK
    n = S // C

    def blocks(t):
        t = t.reshape((B, n, C, H) + t.shape[3:])
        return jnp.swapaxes(t, 2, 3)

    q, k, v, g, b = blocks(q), blocks(k), blocks(v), blocks(log_alpha), blocks(beta)
    gamma = jnp.cumsum(g, axis=-1)
    pos = jnp.arange(C)
    incl = pos[:, None] >= pos[None, :]
    strict = pos[:, None] > pos[None, :]
    diff = gamma[..., :, None] - gamma[..., None, :]
    decay = jnp.where(incl, jnp.exp(jnp.where(incl, diff, 0.0)), 0.0)
    kb = k * b[..., None]
    a_strict = jnp.where(strict, jnp.einsum('bnhid,bnhjd->bnhij', kb, k) * decay, 0.0)
    lhs = a_strict + jnp.eye(C, dtype=jnp.float32)
    rhs = jnp.concatenate([v * b[..., None], kb * jnp.exp(gamma)[..., None]], axis=-1)
    uw = lax.linalg.triangular_solve(lhs, rhs, left_side=True, lower=True, unit_diagonal=True)
    u, w = uw[..., :DV], uw[..., DV:]
    qk = jnp.einsum('bnhid,bnhjd->bnhij', q, k) * decay
    q_dec = q * jnp.exp(gamma)[..., None]
    k_dec = k * jnp.exp(gamma[..., -1:] - gamma)[..., None]
    chunk_decay = jnp.exp(gamma[..., -1])
    xs = tuple(jnp.moveaxis(t, 1, 0) for t in (u, w, qk, q_dec, k_dec, chunk_decay))

    def step(state, inp):
        u_c, w_c, qk_c, qd_c, kd_c, cd_c = inp
        v_new = u_c - jnp.einsum('bhck,bhkv->bhcv', w_c, state)
        o_c = jnp.einsum('bhck,bhkv->bhcv', qd_c, state) + jnp.einsum('bhij,bhjv->bhiv', qk_c, v_new)
        state = state * cd_c[..., None, None] + jnp.einsum('bhck,bhcv->bhkv', kd_c, v_new)
        return state, o_c

    state0 = jnp.zeros((B, H, DK, DV), jnp.float32)
    _, o = lax.scan(step, state0, xs)
    return jnp.transpose(o, (1, 0, 3, 2, 4)).reshape(B, S, H, DV)


def gated_deltanet_branch(qkv, alpha_logit, beta_logit, z, conv_w, a_log, dt_bias, o_norm):
    B, S, _ = qkv.shape
    f32 = jnp.float32
    qkv = jax.nn.silu(centred_depthwise_conv(qkv, conv_w)).astype(f32)
    q, k, v = jnp.split(qkv, 3, axis=-1)
    q = l2_normalize(q.reshape(B, S, H_A, DK_A)) * (DK_A ** -0.5)
    k = l2_normalize(k.reshape(B, S, H_A, DK_A))
    v = v.reshape(B, S, H_A, DV_A)
    dt = jax.nn.softplus(alpha_logit.astype(f32).reshape(B, S, 2, H_A) + dt_bias.astype(f32))
    log_alpha = -jnp.exp(a_log.astype(f32)) * dt
    beta = jax.nn.sigmoid(beta_logit.astype(f32).reshape(B, S, 2, H_A))
    o_fwd = chunked_delta_rule(q, k, v, log_alpha[:, :, 0], beta[:, :, 0])
    rev = lambda t: jnp.flip(t, axis=1)
    o_bwd = rev(chunked_delta_rule(rev(q), rev(k), rev(v), rev(log_alpha[:, :, 1]), rev(beta[:, :, 1])))
    o = rms_norm(o_fwd + o_bwd, o_norm) * jax.nn.silu(z.astype(f32).reshape(B, S, H_A, DV_A))
    return o.reshape(B, S, H_A * DV_A).astype(z.dtype)


def spatial_gating_branch(u, v, sg_norm, sg_w, sg_b):
    B, S, _ = u.shape
    n = S // SG_CHUNK
    u = jax.nn.gelu(u)
    v = rms_norm(jax.nn.gelu(v), sg_norm).reshape(B, n, SG_CHUNK, SG_GROUPS, SG_WIDTH // SG_GROUPS)
    mixed = jnp.einsum('gpq,bnqgc->bnpgc', sg_w, v) + sg_b.T[:, :, None]
    return u * mixed.reshape(B, S, SG_WIDTH)


def rope_tables(S):
    inv_freq = ROPE_THETA ** (-jnp.arange(0, ROPE_DIM, 2, dtype=jnp.float32) / ROPE_DIM)
    ang = jnp.arange(S, dtype=jnp.float32)[:, None] * inv_freq[None, :]
    return jnp.cos(ang), jnp.sin(ang)


def apply_rope(x, cos, sin):
    xf = x.astype(jnp.float32)
    x1, x2 = xf[..., :ROPE_DIM // 2], xf[..., ROPE_DIM // 2:]
    return jnp.concatenate([x1 * cos - x2 * sin, x2 * cos + x1 * sin], axis=-1).astype(x.dtype)


def mla_branch(c_q, c_kv, k_rope, q_a_norm, w_uq, kv_a_norm, w_ukv, q_nope_norm, q_rope_norm, k_nope_norm, k_rope_norm):
    B, S, _ = c_q.shape
    q = (rms_norm(c_q, q_a_norm) @ w_uq).reshape(B, S, H_C, QK_HEAD_DIM)
    kv = (rms_norm(c_kv, kv_a_norm) @ w_ukv).reshape(B, S, H_C, NOPE_DIM + DV_C)
    cos, sin = rope_tables(S)
    q_nope = rms_norm(q[..., :NOPE_DIM], q_nope_norm)
    q_rope = apply_rope(rms_norm(q[..., NOPE_DIM:], q_rope_norm), cos[:, None, :], sin[:, None, :])
    k_nope = rms_norm(kv[..., :NOPE_DIM], k_nope_norm)
    v = kv[..., NOPE_DIM:]
    k_rope = apply_rope(rms_norm(k_rope, k_rope_norm), cos, sin)
    q_all = jnp.concatenate([q_nope, q_rope], axis=-1)
    nb = S // Q_BLOCK
    q_blocks = jnp.moveaxis(q_all.reshape(B, nb, Q_BLOCK, H_C, QK_HEAD_DIM), 1, 0)

    def attend(qb):
        s = (jnp.einsum('bqhd,bkhd->bhqk', qb[..., :NOPE_DIM], k_nope)
             + jnp.einsum('bqhr,bkr->bhqk', qb[..., NOPE_DIM:], k_rope))
        p = jax.nn.softmax(s.astype(jnp.float32) * SM_SCALE, axis=-1)
        return jnp.einsum('bhqk,bkhv->bqhv', p.astype(v.dtype), v)

    o = lax.map(attend, q_blocks)
    return jnp.moveaxis(o, 0, 1).reshape(B, S, H_C * DV_C)


def peer_layer(x, wq, keys, u_tab, v_tab):
    B, S, D = x.shape
    x_blocks = x.reshape(-1, PEER_BLOCK, D)

    def retrieve(xt):
        q = (xt @ wq).reshape(PEER_BLOCK, PEER_HEADS, 2, PEER_QDIM // 2)
        s = jnp.einsum('thpd,hpkd->thpk', q, keys).astype(jnp.float32)
        top_s, top_i = lax.top_k(s, PEER_TOPK)
        cand_s = (top_s[:, :, 0, :, None] + top_s[:, :, 1, None, :]).reshape(PEER_BLOCK, PEER_HEADS, PEER_TOPK * PEER_TOPK)
        cand_i = (top_i[:, :, 0, :, None] * N_KEYS + top_i[:, :, 1, None, :]).reshape(PEER_BLOCK, PEER_HEADS, PEER_TOPK * PEER_TOPK)
        best_s, pos = lax.top_k(cand_s, PEER_TOPK)
        expert = jnp.take_along_axis(cand_i, pos, axis=-1)
        g = jax.nn.softmax(best_s, axis=-1)
        u_e = jnp.take(u_tab, expert, axis=0)
        h = jax.nn.gelu(jnp.einsum('thkd,td->thk', u_e, xt).astype(jnp.float32))
        v_e = jnp.take(v_tab, expert, axis=0)
        return jnp.einsum('thk,thkd->td', (g * h).astype(v_tab.dtype), v_e)

    return lax.map(retrieve, x_blocks).reshape(B, S, D)


def trunk_layer(x, mix_norm, w_in, conv_w, a_log, dt_bias, o_norm, sg_norm, sg_w, sg_b, q_a_norm, w_uq, kv_a_norm, w_ukv,
                q_nope_norm, q_rope_norm, k_nope_norm, k_rope_norm, w_branch, w_out, ffn_norm, peer_wq, peer_keys, peer_u, peer_v):
    B, S, _ = x.shape
    xn = rms_norm(x, mix_norm)
    proj = xn @ w_in
    split_points = np.cumsum(IN_SPLIT_SIZES)[:-1].tolist()
    qkv_a, alpha_a, beta_a, z_a, u_b, v_b, c_q, c_kv, k_rope, gate_logit = jnp.split(proj, split_points, axis=-1)
    out_a = gated_deltanet_branch(qkv_a, alpha_a, beta_a, z_a, conv_w, a_log, dt_bias, o_norm)
    out_b = spatial_gating_branch(u_b, v_b, sg_norm, sg_w, sg_b)
    out_c = mla_branch(c_q, c_kv, k_rope, q_a_norm, w_uq, kv_a_norm, w_ukv, q_nope_norm, q_rope_norm, k_nope_norm, k_rope_norm)
    gates = jax.nn.sigmoid(gate_logit.astype(jnp.float32)).reshape(B, S, N_BRANCH, D_MODEL)
    merged = (gates[:, :, 0] * (out_a @ w_branch[0]).astype(jnp.float32)
              + gates[:, :, 1] * (out_b @ w_branch[1]).astype(jnp.float32)
              + gates[:, :, 2] * (out_c @ w_branch[2]).astype(jnp.float32))
    h = x + merged.astype(x.dtype) @ w_out
    return h + peer_layer(rms_norm(h, ffn_norm), peer_wq, peer_keys, peer_u, peer_v)


def setup_inputs(seed: int = 0) -> dict:
    key = jax.random.key(seed)
    ks = jax.random.split(key, 32)
    L = DEPTH
    f32 = jnp.float32

    def nrm(k, shape, scale):
        return jax.random.normal(k, shape, f32) * scale

    def gain(k, n):
        return 1.0 + 0.02 * jax.random.normal(k, (L, n), f32)

    dt = jnp.exp(jax.random.uniform(ks[5], (L, 2, H_A), f32, math.log(1e-3), math.log(1e-1)))
    return {
        'x_prompt': nrm(ks[0], (BATCH, SEQ, D_MODEL), 1.0),
        'x_sample': nrm(ks[1], (DEC_BATCH, DEC_SEQ, D_MODEL), 1.0),
        'mix_norm': gain(ks[2], D_MODEL),
        'w_in': nrm(ks[3], (L, D_MODEL, IN_COLS), D_MODEL ** -0.5),
        'conv_w': nrm(ks[4], (L, CONV_W, QKV_A_COLS), CONV_W ** -0.5),
        'a_log': jnp.log(jax.random.uniform(ks[6], (L, 2, H_A), f32, 1.0, 16.0)),
        'dt_bias': dt + jnp.log(-jnp.expm1(-dt)),
        'o_norm': gain(ks[7], DV_A),
        'sg_norm': gain(ks[8], SG_WIDTH),
        'sg_w': nrm(ks[9], (L, SG_GROUPS, SG_CHUNK, SG_CHUNK), SG_CHUNK ** -0.5),
        'sg_b': nrm(ks[10], (L, SG_GROUPS, SG_CHUNK), 0.02),
        'q_a_norm': gain(ks[11], Q_LORA),
        'w_uq': nrm(ks[12], (L, Q_LORA, H_C * QK_HEAD_DIM), Q_LORA ** -0.5),
        'kv_a_norm': gain(ks[13], KV_LORA),
        'w_ukv': nrm(ks[14], (L, KV_LORA, H_C * (NOPE_DIM + DV_C)), KV_LORA ** -0.5),
        'q_nope_norm': gain(ks[15], NOPE_DIM),
        'q_rope_norm': gain(ks[16], ROPE_DIM),
        'k_nope_norm': gain(ks[17], NOPE_DIM),
        'k_rope_norm': gain(ks[18], ROPE_DIM),
        'w_branch': nrm(ks[19], (L, N_BRANCH, BRANCH_WIDTH, D_MODEL), BRANCH_WIDTH ** -0.5),
        'w_out': nrm(ks[20], (L, D_MODEL, D_MODEL), D_MODEL ** -0.5),
        'ffn_norm': gain(ks[21], D_MODEL),
        'peer_wq': nrm(ks[22], (L, D_MODEL, PEER_HEADS * PEER_QDIM), D_MODEL ** -0.5),
        'peer_keys': nrm(ks[23], (L, PEER_HEADS, 2, N_KEYS, PEER_QDIM // 2), (PEER_QDIM // 2) ** -0.5),
        'peer_u': nrm(ks[24], (L, N_EXPERTS, D_MODEL), D_MODEL ** -0.5),
        'peer_v': nrm(ks[25], (L, N_EXPERTS, D_MODEL), (PEER_HEADS * PEER_TOPK) ** -0.5),
    }


def reference(x_prompt, x_sample, mix_norm, w_in, conv_w, a_log, dt_bias, o_norm, sg_norm, sg_w, sg_b, q_a_norm, w_uq,
              kv_a_norm, w_ukv, q_nope_norm, q_rope_norm, k_nope_norm, k_rope_norm, w_branch, w_out, ffn_norm,
              peer_wq, peer_keys, peer_u, peer_v):
    def run_trunk(x):
        for l in range(DEPTH):
            x = trunk_layer(x, mix_norm[l], w_in[l], conv_w[l], a_log[l], dt_bias[l], o_norm[l], sg_norm[l], sg_w[l], sg_b[l],
                            q_a_norm[l], w_uq[l], kv_a_norm[l], w_ukv[l], q_nope_norm[l], q_rope_norm[l], k_nope_norm[l],
                            k_rope_norm[l], w_branch[l], w_out[l], ffn_norm[l], peer_wq[l], peer_keys[l], peer_u[l], peer_v[l])
        return x

    y_prompt = run_trunk(x_prompt)
    y_sample = run_trunk(x_sample)
    return (y_prompt, y_sample)
```

```python
import functools
import math

import jax
import jax.numpy as jnp
from jax import lax
from jax.experimental import pallas as pl
from jax.experimental.pallas import tpu as pltpu

D_MODEL = 1024
NORM_EPS = 1e-6
H_A = 4
DK_A = 128
DV_A = 128
CONV_W = 5
DELTA_CHUNK = 64
QKV_A_COLS = 3 * H_A * DK_A
SG_GROUPS = 4
SG_CHUNK = 128
SG_WIDTH = 512
H_C = 4
Q_LORA = 384
KV_LORA = 256
NOPE_DIM = 128
ROPE_DIM = 64
DV_C = 128
QK_HEAD_DIM = NOPE_DIM + ROPE_DIM
ROPE_THETA = 10000.0
SM_SCALE = QK_HEAD_DIM ** -0.5
PEER_HEADS = 8
N_KEYS = 128
PEER_QDIM = 256
PEER_TOPK = 16
N_BRANCH = 3

MXU_DTYPE = jnp.bfloat16
LANES = 128
VMEM_LIMIT = 56 * 1024 * 1024

_IN_GROUPS = (("qkv", QKV_A_COLS), ("ab", LANES), ("z", 512), ("u", SG_WIDTH), ("v", SG_WIDTH),
              ("cq", Q_LORA), ("ckv", KV_LORA), ("kr", LANES), ("gate", N_BRANCH * D_MODEL))
_IN_OFF = {}
_off = 0
for _name, _w in _IN_GROUPS:
    _IN_OFF[_name] = (_off, _off + _w)
    _off += _w
IN_COLS_PAD = _off

F32 = jnp.float32


def _mm(a, b):
    return jnp.dot(a.astype(MXU_DTYPE), b.astype(MXU_DTYPE), preferred_element_type=F32)


def _mm_nt(a, b):
    return lax.dot_general(a.astype(MXU_DTYPE), b.astype(MXU_DTYPE), (((1,), (1,)), ((), ())),
                           preferred_element_type=F32)


def _mm_tn(a, b):
    return lax.dot_general(a.astype(MXU_DTYPE), b.astype(MXU_DTYPE), (((0,), (0,)), ((), ())),
                           preferred_element_type=F32)


def _rms(x, gain):
    return x * lax.rsqrt(jnp.mean(x * x, axis=-1, keepdims=True) + NORM_EPS) * gain


def _sigmoid(x):
    return 1.0 / (1.0 + jnp.exp(-x))


def _silu(x):
    return x * _sigmoid(x)


def _gelu(x):
    return x * (0.5 * (1.0 + jnp.tanh(math.sqrt(2.0 / math.pi) * (x + 0.044715 * (x * x * x)))))


def _softplus(x):
    return jnp.maximum(x, 0.0) + jnp.log1p(jnp.exp(-jnp.abs(x)))


def _cparams(sem):
    return pltpu.CompilerParams(dimension_semantics=sem, vmem_limit_bytes=VMEM_LIMIT)


def _vmem_full():
    return pl.BlockSpec(memory_space=pltpu.VMEM)


def _inproj_kernel(x_ref, cos_ref, sin_ref, mixg_ref, win_ref, alog_ref, dtb_ref, sgn_ref, qan_ref, wuq_ref,
                   kvan_ref, wukv_ref, qnn_ref, qrn_ref, knn_ref, krn_ref,
                   qkv_ref, gam_ref, sz_ref, ub_ref, vb_ref, qc_ref, kc_ref, vc_ref, gate_ref):
    tm = x_ref.shape[0]
    xn = _rms(x_ref[...], mixg_ref[...]).astype(MXU_DTYPE)

    def proj(name):
        a, b = _IN_OFF[name]
        return jnp.dot(xn, win_ref[:, a:b], preferred_element_type=F32)

    qkv_ref[...] = proj("qkv")
    sz_ref[...] = _silu(proj("z"))
    ub_ref[...] = _gelu(proj("u"))
    vb_ref[...] = _rms(_gelu(proj("v")), sgn_ref[...]).astype(vb_ref.dtype)
    gate_ref[...] = _sigmoid(proj("gate"))

    ab = proj("ab")
    col = lax.broadcasted_iota(jnp.int32, ab.shape, 1)
    la = jnp.where(col < 2 * H_A, -jnp.exp(alog_ref[...]) * _softplus(ab + dtb_ref[...]), 0.0)
    beta = _sigmoid(ab)
    r = lax.broadcasted_iota(jnp.int32, (tm, tm), 0)
    c = lax.broadcasted_iota(jnp.int32, (tm, tm), 1)
    same = (r // DELTA_CHUNK) == (c // DELTA_CHUNK)
    tri_f = jnp.where(same & (r >= c), 1.0, 0.0).astype(MXU_DTYPE)
    tri_b = jnp.where(same & (r <= c), 1.0, 0.0).astype(MXU_DTYPE)
    p0 = la.astype(MXU_DTYPE)
    r1 = la - p0.astype(F32)
    p1 = r1.astype(MXU_DTYPE)
    p2 = (r1 - p1.astype(F32)).astype(MXU_DTYPE)
    gf = (jnp.dot(tri_f, p0, preferred_element_type=F32) + jnp.dot(tri_f, p1, preferred_element_type=F32)
          + jnp.dot(tri_f, p2, preferred_element_type=F32))
    gb = (jnp.dot(tri_b, p0, preferred_element_type=F32) + jnp.dot(tri_b, p1, preferred_element_type=F32)
          + jnp.dot(tri_b, p2, preferred_element_type=F32))
    gam = jnp.where(col < H_A, gf, gb)
    gam_ref[...] = jnp.where(col < 2 * H_A, gam, jnp.where(col < 4 * H_A, beta, 0.0))

    cos2 = cos_ref[...]
    sin2 = sin_ref[...]

    def rope(t):
        swapped = jnp.concatenate([t[:, ROPE_DIM // 2:], t[:, :ROPE_DIM // 2]], axis=-1)
        return t * cos2 + swapped * sin2

    q = _mm(_rms(proj("cq"), qan_ref[...]), wuq_ref[...])
    kv = _mm(_rms(proj("ckv"), kvan_ref[...]), wukv_ref[...])
    kr = rope(_rms(proj("kr")[:, :ROPE_DIM], krn_ref[...]))
    for h in range(H_C):
        qn = _rms(q[:, h * NOPE_DIM:(h + 1) * NOPE_DIM], qnn_ref[...])
        o = H_C * NOPE_DIM + h * ROPE_DIM
        qr = rope(_rms(q[:, o:o + ROPE_DIM], qrn_ref[...]))
        qc_ref[h] = (jnp.concatenate([qn, qr], axis=-1) * SM_SCALE).astype(qc_ref.dtype)
        kn = _rms(kv[:, h * 256:h * 256 + NOPE_DIM], knn_ref[...])
        kc_ref[h] = jnp.concatenate([kn, kr], axis=-1).astype(kc_ref.dtype)
        vc_ref[h] = kv[:, h * 256 + NOPE_DIM:(h + 1) * 256].astype(vc_ref.dtype)


def _inproj(x, seq, cos2, sin2, lw):
    T = x.shape[0]
    tm = min(256, seq)
    nseq = seq // tm
    row = lambda i: (i, 0)
    full2 = lambda shape: pl.BlockSpec(shape, lambda i: (0, 0))
    in_specs = [
        pl.BlockSpec((tm, D_MODEL), row),
        pl.BlockSpec((tm, ROPE_DIM), lambda i: (i % nseq, 0)),
        pl.BlockSpec((tm, ROPE_DIM), lambda i: (i % nseq, 0)),
        full2((1, D_MODEL)),
        _vmem_full(),
        full2((1, LANES)), full2((1, LANES)), full2((1, SG_WIDTH)), full2((1, Q_LORA)),
        _vmem_full(),
        full2((1, KV_LORA)),
        _vmem_full(),
        full2((1, NOPE_DIM)), full2((1, ROPE_DIM)), full2((1, NOPE_DIM)), full2((1, ROPE_DIM)),
    ]
    head3 = lambda w: pl.BlockSpec((H_C, tm, w), lambda i: (0, i, 0))
    out_shape = (
        jax.ShapeDtypeStruct((T, QKV_A_COLS), F32),
        jax.ShapeDtypeStruct((T, LANES), F32),
        jax.ShapeDtypeStruct((T, 512), F32),
        jax.ShapeDtypeStruct((T, SG_WIDTH), F32),
        jax.ShapeDtypeStruct((T, SG_WIDTH), MXU_DTYPE),
        jax.ShapeDtypeStruct((H_C, T, QK_HEAD_DIM), MXU_DTYPE),
        jax.ShapeDtypeStruct((H_C, T, QK_HEAD_DIM), MXU_DTYPE),
        jax.ShapeDtypeStruct((H_C, T, DV_C), MXU_DTYPE),
        jax.ShapeDtypeStruct((T, N_BRANCH * D_MODEL), F32),
    )
    out_specs = (
        pl.BlockSpec((tm, QKV_A_COLS), row), pl.BlockSpec((tm, LANES), row), pl.BlockSpec((tm, 512), row),
        pl.BlockSpec((tm, SG_WIDTH), row), pl.BlockSpec((tm, SG_WIDTH), row),
        head3(QK_HEAD_DIM), head3(QK_HEAD_DIM), head3(DV_C),
        pl.BlockSpec((tm, N_BRANCH * D_MODEL), row),
    )
    return pl.pallas_call(
        _inproj_kernel, grid=(T // tm,), in_specs=in_specs, out_specs=out_specs, out_shape=out_shape,
        compiler_params=_cparams(("parallel",)),
    )(x, cos2, sin2, lw["mix_norm"], lw["w_in"], lw["a_log"], lw["dt_bias"], lw["sg_norm"], lw["q_a_norm"],
      lw["w_uq"], lw["kv_a_norm"], lw["w_ukv"], lw["q_nope_norm"], lw["q_rope_norm"], lw["k_nope_norm"],
      lw["k_rope_norm"])


_HALO = 8


def _conv_kernel(prev_ref, x_ref, next_ref, w_ref, q_ref, k_ref, v_ref, ext_ref):
    i = pl.program_id(1)
    n = pl.num_programs(1)
    ts = x_ref.shape[1]
    ext_ref[0:_HALO, :] = jnp.where(i > 0, prev_ref[0], 0.0)
    ext_ref[_HALO:_HALO + ts, :] = x_ref[0]
    ext_ref[_HALO + ts:, :] = jnp.where(i < n - 1, next_ref[0], 0.0)
    acc = None
    for j in range(CONV_W):
        start = _HALO - CONV_W // 2 + j
        term = ext_ref[start:start + ts, :] * w_ref[j:j + 1, :]
        acc = term if acc is None else acc + term
    y = _silu(acc)
    for h in range(H_A):
        qh = y[:, h * DK_A:(h + 1) * DK_A]
        kh = y[:, (H_A + h) * DK_A:(H_A + h + 1) * DK_A]
        q_ref[0, :, h * DK_A:(h + 1) * DK_A] = qh * (
            lax.rsqrt(jnp.sum(qh * qh, axis=-1, keepdims=True) + NORM_EPS) * (DK_A ** -0.5))
        k_ref[0, :, h * DK_A:(h + 1) * DK_A] = kh * lax.rsqrt(jnp.sum(kh * kh, axis=-1, keepdims=True) + NORM_EPS)
    v_ref[0] = y[:, 2 * H_A * DK_A:]


def _conv_prep(qkv, conv_w):
    B, S, C = qkv.shape
    ts = min(512, S)
    nb = S // ts
    hb = ts // _HALO
    nh = S // _HALO
    out = jax.ShapeDtypeStruct((B, S, H_A * DK_A), F32)
    blk = pl.BlockSpec((1, ts, H_A * DK_A), lambda b, i: (b, i, 0))
    return pl.pallas_call(
        _conv_kernel, grid=(B, nb),
        in_specs=[
            pl.BlockSpec((1, _HALO, C), lambda b, i: (b, jnp.maximum(i * hb - 1, 0), 0)),
            pl.BlockSpec((1, ts, C), lambda b, i: (b, i, 0)),
            pl.BlockSpec((1, _HALO, C), lambda b, i: (b, jnp.minimum((i + 1) * hb, nh - 1), 0)),
            pl.BlockSpec((CONV_W, C), lambda b, i: (0, 0)),
        ],
        out_specs=(blk, blk, blk), out_shape=(out, out, out),
        scratch_shapes=[pltpu.VMEM((ts + 2 * _HALO, C), F32)],
        compiler_params=_cparams(("parallel", "parallel")),
    )(qkv, qkv, qkv, conv_w)


def _delta_unit(q, k, v, gam_c, gam_r, b_c, state, backward):
    C = DELTA_CHUNK
    r = lax.broadcasted_iota(jnp.int32, (C, C), 0)
    c = lax.broadcasted_iota(jnp.int32, (C, C), 1)
    if backward:
        incl, strict, last = r <= c, r < c, 0
    else:
        incl, strict, last = r >= c, r > c, C - 1
    decay = jnp.where(incl, jnp.exp(jnp.where(incl, gam_c - gam_r, 0.0)), 0.0)
    kb = k * b_c
    a = jnp.where(strict, _mm_nt(kb, k) * decay, 0.0)
    eg = jnp.exp(gam_c)
    y = jnp.concatenate([v * b_c, kb * eg], axis=-1)
    y = y - _mm(a, y)
    p = a
    for _ in range(5):
        p = _mm(p, p)
        y = y + _mm(p, y)
    u, w = y[:, :DV_A], y[:, DV_A:]
    qk = _mm_nt(q, k) * decay
    g_last = gam_c[last:last + 1, :]
    k_dec = k * jnp.exp(g_last - gam_c)
    v_new = u - _mm(w, state)
    o = _mm(q * eg, state) + _mm(qk, v_new)
    new_state = state * jnp.exp(g_last) + _mm_tn(k_dec, v_new)
    return o, new_state


def _delta_kernel(qf_ref, kf_ref, vf_ref, gf_ref, rf_ref, qb_ref, kb_ref, vb_ref, gb_ref, rb_ref,
                  of_ref, ob_ref, state_ref):
    C = DELTA_CHUNK
    nc = qf_ref.shape[1] // C

    @pl.when(pl.program_id(1) == 0)
    def _():
        state_ref[...] = jnp.zeros_like(state_ref)

    def body(ci, carry):
        for d, (q_ref, k_ref, v_ref, g_ref, r_ref, o_ref) in enumerate(
                ((qf_ref, kf_ref, vf_ref, gf_ref, rf_ref, of_ref), (qb_ref, kb_ref, vb_ref, gb_ref, rb_ref, ob_ref))):
            cc = ci if d == 0 else nc - 1 - ci
            rows = pl.ds(pl.multiple_of(cc * C, C), C)
            g = g_ref[0, rows, :]
            grow = r_ref[0, cc]
            for h in range(H_A):
                lanes = slice(h * DK_A, (h + 1) * DK_A)
                j = d * H_A + h
                o, s_new = _delta_unit(q_ref[0, rows, lanes], k_ref[0, rows, lanes], v_ref[0, rows, lanes],
                                       g[:, j:j + 1], grow[j:j + 1, :], g[:, 2 * H_A + j:2 * H_A + j + 1],
                                       state_ref[j], backward=(d == 1))
                o_ref[0, rows, lanes] = o
                state_ref[j] = s_new
        return carry

    lax.fori_loop(0, nc, body, 0)


def _delta_rule(q, k, v, gam, gam_rows):
    B, S, W = q.shape
    ts = min(256, S)
    nb = S // ts
    ncb = ts // DELTA_CHUNK
    fwd = lambda b, i: (b, i, 0)
    bwd = lambda b, i: (b, nb - 1 - i, 0)
    fwd4 = lambda b, i: (b, i, 0, 0)
    bwd4 = lambda b, i: (b, nb - 1 - i, 0, 0)

    def specs(m3, m4):
        return [pl.BlockSpec((1, ts, W), m3), pl.BlockSpec((1, ts, W), m3), pl.BlockSpec((1, ts, W), m3),
                pl.BlockSpec((1, ts, LANES), m3), pl.BlockSpec((1, ncb, 2 * H_A, DELTA_CHUNK), m4)]

    out = jax.ShapeDtypeStruct((B, S, W), F32)
    return pl.pallas_call(
        _delta_kernel, grid=(B, nb), in_specs=specs(fwd, fwd4) + specs(bwd, bwd4),
        out_specs=(pl.BlockSpec((1, ts, W), fwd), pl.BlockSpec((1, ts, W), bwd)), out_shape=(out, out),
        scratch_shapes=[pltpu.VMEM((2 * H_A, DK_A, DV_A), F32)],
        compiler_params=_cparams(("parallel", "arbitrary")),
    )(q, k, v, gam, gam_rows, q, k, v, gam, gam_rows)


def _attn_kernel(q_ref, k_ref, v_ref, o_ref, m_ref, l_ref, acc_ref):
    kv = pl.program_id(3)

    @pl.when(kv == 0)
    def _():
        m_ref[...] = jnp.full_like(m_ref, -jnp.inf)
        l_ref[...] = jnp.zeros_like(l_ref)
        acc_ref[...] = jnp.zeros_like(acc_ref)

    s = _mm_nt(q_ref[0, 0], k_ref[0, 0])
    m_prev = m_ref[...]
    m_new = jnp.maximum(m_prev, jnp.max(s, axis=-1, keepdims=True))
    alpha = jnp.exp(m_prev - m_new)
    p = jnp.exp(s - m_new)
    l_ref[...] = alpha * l_ref[...] + jnp.sum(p, axis=-1, keepdims=True)
    acc_ref[...] = alpha * acc_ref[...] + _mm(p, v_ref[0, 0])
    m_ref[...] = m_new

    @pl.when(kv == pl.num_programs(3) - 1)
    def _():
        o_ref[0] = (acc_ref[...] / l_ref[...]).astype(o_ref.dtype)


def _attention(qc, kc, vc, B, S):
    tq = min(512, S)
    tk = min(1024, S)
    qc = qc.reshape(H_C, B, S, QK_HEAD_DIM)
    kc = kc.reshape(H_C, B, S, QK_HEAD_DIM)
    vc = vc.reshape(H_C, B, S, DV_C)
    return pl.pallas_call(
        _attn_kernel, grid=(B, H_C, S // tq, S // tk),
        in_specs=[pl.BlockSpec((1, 1, tq, QK_HEAD_DIM), lambda b, h, i, j: (h, b, i, 0)),
                  pl.BlockSpec((1, 1, tk, QK_HEAD_DIM), lambda b, h, i, j: (h, b, j, 0)),
                  pl.BlockSpec((1, 1, tk, DV_C), lambda b, h, i, j: (h, b, j, 0))],
        out_specs=pl.BlockSpec((1, tq, DV_C), lambda b, h, i, j: (b, i, h)),
        out_shape=jax.ShapeDtypeStruct((B, S, H_C * DV_C), MXU_DTYPE),
        scratch_shapes=[pltpu.VMEM((tq, 1), F32), pltpu.VMEM((tq, 1), F32), pltpu.VMEM((tq, DV_C), F32)],
        compiler_params=_cparams(("parallel", "parallel", "parallel", "arbitrary")),
    )(qc, kc, vc)


def _merge_kernel(x_ref, of_ref, ob_ref, sz_ref, ub_ref, vb_ref, oc_ref, gate_ref, onorm_ref, sgw_ref, sgb_ref,
                  wbr_ref, wout_ref, h_ref):
    tm = x_ref.shape[0]
    o = of_ref[...] + ob_ref[...]
    sz = sz_ref[...]
    out_a = jnp.concatenate(
        [_rms(o[:, h * DV_A:(h + 1) * DV_A], onorm_ref[...]) * sz[:, h * DV_A:(h + 1) * DV_A] for h in range(H_A)],
        axis=-1)
    gw = SG_WIDTH // SG_GROUPS
    chunks = []
    for ci in range(tm // SG_CHUNK):
        rows = slice(ci * SG_CHUNK, (ci + 1) * SG_CHUNK)
        mixed = jnp.concatenate(
            [jnp.dot(sgw_ref[g], vb_ref[rows, g * gw:(g + 1) * gw], preferred_element_type=F32) + sgb_ref[:, g:g + 1]
             for g in range(SG_GROUPS)], axis=-1)
        chunks.append(ub_ref[rows, :] * mixed)
    out_b = jnp.concatenate(chunks, axis=0)
    gate = gate_ref[...]
    merged = (gate[:, :D_MODEL] * _mm(out_a, wbr_ref[0])
              + gate[:, D_MODEL:2 * D_MODEL] * _mm(out_b, wbr_ref[1])
              + gate[:, 2 * D_MODEL:] * jnp.dot(oc_ref[...], wbr_ref[2], preferred_element_type=F32))
    h_ref[...] = x_ref[...] + _mm(merged, wout_ref[...])


def _merge(x, o_f, o_b, sz, ub, vb, oc, gate, lw):
    T = x.shape[0]
    tm = min(256, T)
    row = lambda i: (i, 0)
    w512 = pl.BlockSpec((tm, 512), row)
    return pl.pallas_call(
        _merge_kernel, grid=(T // tm,),
        in_specs=[pl.BlockSpec((tm, D_MODEL), row), w512, w512, w512, w512, w512, w512,
                  pl.BlockSpec((tm, N_BRANCH * D_MODEL), row),
                  pl.BlockSpec((1, DV_A), lambda i: (0, 0)),
                  _vmem_full(), _vmem_full(), _vmem_full(), _vmem_full()],
        out_specs=pl.BlockSpec((tm, D_MODEL), row),
        out_shape=jax.ShapeDtypeStruct((T, D_MODEL), F32),
        compiler_params=_cparams(("parallel",)),
    )(x, o_f, o_b, sz, ub, vb, oc, gate, lw["o_norm"], lw["sg_w"], lw["sg_bT"], lw["w_branch"], lw["w_out"])


_NEG_INF = float("-inf")
_CAND_B_LIMIT = tuple(PEER_TOPK // (a + 1) for a in range(PEER_TOPK))


def _top16_rows(s, top_ref):
    n, tt = s.shape
    rid = lax.broadcasted_iota(jnp.int32, (n, tt), 0)
    work = s
    pos = jnp.full((n, tt), float(PEER_TOPK), F32)
    for r in range(PEER_TOPK):
        m = jnp.max(work, axis=0, keepdims=True)
        idx = jnp.min(jnp.where(work == m, rid, n), axis=0, keepdims=True)
        sel = rid == idx
        pos = jnp.where(sel, float(r), pos)
        work = jnp.where(sel, _NEG_INF, work)
        top_ref[r:r + 1, :] = m
    return pos


def _peer_topk_kernel(h_ref, gain_ref, wq_ref, keys_ref, xn_ref, pos1_ref, bm_ref, nbrow_ref, arow_ref,
                      q_ref, t0_ref, t1_ref):
    tt = h_ref.shape[0]
    xn = _rms(h_ref[...], gain_ref[...]).astype(MXU_DTYPE)
    xn_ref[...] = xn
    q_ref[...] = jnp.dot(xn, wq_ref[...], preferred_element_type=F32).astype(MXU_DTYPE)
    sub = lax.broadcasted_iota(jnp.int32, (8, tt), 0)

    def head(h, carry):
        base = pl.multiple_of(h * PEER_QDIM, PEER_QDIM)
        s0 = _mm_nt(keys_ref[h, 0], q_ref[:, pl.ds(base, N_KEYS)])
        s1 = _mm_nt(keys_ref[h, 1], q_ref[:, pl.ds(base + N_KEYS, N_KEYS)])
        pos0 = _top16_rows(s0, t0_ref)
        pos1 = _top16_rows(s1, t1_ref)
        t0 = t0_ref[...]
        t1 = t1_ref[...]
        t00 = t0[0:1, :]
        t10 = t1[0:1, :]
        c00 = t00 + t10
        vals = [t00 + t1[0:8, :], t00 + t1[8:16, :]]
        cidx = [sub, sub + 8]
        for a in range(1, 8):
            vals.append(jnp.where(sub < _CAND_B_LIMIT[a], t0[a:a + 1, :] + t1[0:8, :], _NEG_INF))
            cidx.append(sub + a * PEER_TOPK)
        vals.append(t0[8:16, :] + t10)
        cidx.append((sub + 8) * PEER_TOPK)
        orig = list(vals)
        big = PEER_TOPK * PEER_TOPK
        for _ in range(PEER_TOPK):
            m = functools.reduce(jnp.maximum, vals)
            m = jnp.max(m, axis=0, keepdims=True)
            cand = functools.reduce(jnp.minimum, [jnp.where(v == m, ci, big) for v, ci in zip(vals, cidx)])
            idx = jnp.min(cand, axis=0, keepdims=True)
            vals = [jnp.where(ci == idx, _NEG_INF, v) for v, ci in zip(vals, cidx)]
        sel = [jnp.where(v != o, 1.0, 0.0) for v, o in zip(vals, orig)]
        z = functools.reduce(
            lambda x, y: x + y,
            [jnp.sum(jnp.where(s > 0.0, jnp.exp(jnp.where(s > 0.0, o - c00, 0.0)), 0.0), axis=0, keepdims=True)
             for s, o in zip(sel, orig)])
        nb_lo = [jnp.sum(sel[0], axis=0, keepdims=True) + jnp.sum(sel[1], axis=0, keepdims=True)]
        for a in range(1, 8):
            nb_lo.append(jnp.sum(sel[a + 1], axis=0, keepdims=True))
        nb_hi = sel[9]
        nbrow = jnp.zeros((N_KEYS, tt), F32)
        for a in range(PEER_TOPK):
            nba = nb_lo[a] if a < 8 else nb_hi[a - 8:a - 7, :]
            nbrow = jnp.where(pos0 == float(a), nba, nbrow)
        inv_z = 1.0 / z
        nbrow_ref[h] = nbrow
        arow_ref[h] = jnp.where(pos0 < float(PEER_TOPK), jnp.exp(jnp.where(pos0 < float(PEER_TOPK), s0 - t00, 0.0)),
                                0.0) * inv_z
        pos1_ref[h] = pos1
        bm_ref[h] = jnp.where(pos1 < float(PEER_TOPK), jnp.exp(jnp.where(pos1 < float(PEER_TOPK), s1 - t10, 0.0)),
                              0.0)
        return carry

    lax.fori_loop(0, PEER_HEADS, head, 0)


def _peer_topk(h, lw):
    T = h.shape[0]
    tt = min(256, T)
    meta = jax.ShapeDtypeStruct((PEER_HEADS, N_KEYS, T), F32)
    mspec = pl.BlockSpec((PEER_HEADS, N_KEYS, tt), lambda i: (0, 0, i))
    return pl.pallas_call(
        _peer_topk_kernel, grid=(T // tt,),
        in_specs=[pl.BlockSpec((tt, D_MODEL), lambda i: (i, 0)), pl.BlockSpec((1, D_MODEL), lambda i: (0, 0)),
                  _vmem_full(), _vmem_full()],
        out_specs=(pl.BlockSpec((tt, D_MODEL), lambda i: (i, 0)), mspec, mspec, mspec, mspec),
        out_shape=(jax.ShapeDtypeStruct((T, D_MODEL), MXU_DTYPE), meta, meta, meta, meta),
        scratch_shapes=[pltpu.VMEM((tt, PEER_HEADS * PEER_QDIM), MXU_DTYPE), pltpu.VMEM((PEER_TOPK, tt), F32),
                        pltpu.VMEM((PEER_TOPK, tt), F32)],
        compiler_params=_cparams(("parallel",)),
    )(h, lw["ffn_norm"], lw["peer_wq"], lw["peer_keys"])


_PEER_IB = 8


def _peer_dense_kernel(h_ref, xn_ref, u_ref, vt_ref, pos1_ref, bm_ref, nbrow_ref, arow_ref, o_ref, acc_ref, gh_ref):
    ib = pl.program_id(1)

    @pl.when(ib == 0)
    def _():
        acc_ref[...] = jnp.zeros_like(acc_ref)

    xn = xn_ref[...]
    row0 = pl.multiple_of(ib * _PEER_IB, _PEER_IB)
    for il in range(_PEER_IB):
        hid = _mm_nt(u_ref[il * N_KEYS:(il + 1) * N_KEYS, :], xn)
        g = None
        for h in range(PEER_HEADS):
            nb = nbrow_ref[h, pl.ds(row0, _PEER_IB), :][il:il + 1, :]
            ar = arow_ref[h, pl.ds(row0, _PEER_IB), :][il:il + 1, :]
            term = jnp.where(pos1_ref[h] < nb, bm_ref[h], 0.0) * ar
            g = term if g is None else g + term
        gh_ref[il * N_KEYS:(il + 1) * N_KEYS, :] = (g * _gelu(hid)).astype(gh_ref.dtype)
    acc_ref[...] += jnp.dot(vt_ref[...], gh_ref[...], preferred_element_type=F32)

    @pl.when(ib == pl.num_programs(1) - 1)
    def _():
        o_ref[...] = h_ref[...] + acc_ref[...].T


def _peer_dense(h, xn, pos1, bm, nbrow, arow, lw):
    T = h.shape[0]
    tt = min(512, T)
    eb = _PEER_IB * N_KEYS
    mspec = pl.BlockSpec((PEER_HEADS, N_KEYS, tt), lambda i, j: (0, 0, i))
    return pl.pallas_call(
        _peer_dense_kernel, grid=(T // tt, N_KEYS // _PEER_IB),
        in_specs=[pl.BlockSpec((tt, D_MODEL), lambda i, j: (i, 0)), pl.BlockSpec((tt, D_MODEL), lambda i, j: (i, 0)),
                  pl.BlockSpec((eb, D_MODEL), lambda i, j: (j, 0)), pl.BlockSpec((D_MODEL, eb), lambda i, j: (0, j)),
                  mspec, mspec, mspec, mspec],
        out_specs=pl.BlockSpec((tt, D_MODEL), lambda i, j: (i, 0)),
        out_shape=jax.ShapeDtypeStruct((T, D_MODEL), F32),
        scratch_shapes=[pltpu.VMEM((D_MODEL, tt), F32), pltpu.VMEM((eb, tt), MXU_DTYPE)],
        compiler_params=_cparams(("parallel", "arbitrary")),
    )(h, xn, lw["peer_u"], lw["peer_vT"], pos1, bm, nbrow, arow)


def _pad_cols(w, width):
    return jnp.pad(w, ((0, 0), (0, width - w.shape[1])))


def _layer_weights(l, mix_norm, w_in, conv_w, a_log, dt_bias, o_norm, sg_norm, sg_w, sg_b, q_a_norm, w_uq, kv_a_norm,
                   w_ukv, q_nope_norm, q_rope_norm, k_nope_norm, k_rope_norm, w_branch, w_out, ffn_norm, peer_wq,
                   peer_keys, peer_u, peer_v):
    split = [QKV_A_COLS, 2 * H_A, 2 * H_A, H_A * DV_A, SG_WIDTH, SG_WIDTH, Q_LORA, KV_LORA, ROPE_DIM,
             N_BRANCH * D_MODEL]
    offs = [0]
    for s in split:
        offs.append(offs[-1] + s)
    wl = w_in[l]
    piece = lambda i: wl[:, offs[i]:offs[i + 1]]
    w_cat = jnp.concatenate([
        piece(0), _pad_cols(jnp.concatenate([piece(1), piece(2)], axis=1), LANES), piece(3), piece(4), piece(5),
        piece(6), piece(7), _pad_cols(piece(8), LANES), piece(9)], axis=1).astype(MXU_DTYPE)
    wq3 = w_uq[l].reshape(Q_LORA, H_C, QK_HEAD_DIM)
    w_uq_r = jnp.concatenate([wq3[:, :, :NOPE_DIM].reshape(Q_LORA, H_C * NOPE_DIM),
                              wq3[:, :, NOPE_DIM:].reshape(Q_LORA, H_C * ROPE_DIM)], axis=1).astype(MXU_DTYPE)
    row = lambda v: v.reshape(1, -1).astype(F32)
    return {
        "mix_norm": row(mix_norm[l]), "w_in": w_cat, "conv_w": conv_w[l].astype(F32),
        "a_log": _pad_cols(row(a_log[l]), LANES), "dt_bias": _pad_cols(row(dt_bias[l]), LANES),
        "o_norm": row(o_norm[l]), "sg_norm": row(sg_norm[l]), "sg_w": sg_w[l].astype(MXU_DTYPE),
        "sg_bT": sg_b[l].T.astype(F32), "q_a_norm": row(q_a_norm[l]), "w_uq": w_uq_r,
        "kv_a_norm": row(kv_a_norm[l]), "w_ukv": w_ukv[l].astype(MXU_DTYPE), "q_nope_norm": row(q_nope_norm[l]),
        "q_rope_norm": row(q_rope_norm[l]), "k_nope_norm": row(k_nope_norm[l]), "k_rope_norm": row(k_rope_norm[l]),
        "w_branch": w_branch[l].astype(MXU_DTYPE), "w_out": w_out[l].astype(MXU_DTYPE), "ffn_norm": row(ffn_norm[l]),
        "peer_wq": peer_wq[l].astype(MXU_DTYPE), "peer_keys": peer_keys[l].astype(MXU_DTYPE),
        "peer_u": peer_u[l].astype(MXU_DTYPE), "peer_vT": peer_v[l].astype(MXU_DTYPE).T,
    }


def _rope_tables(S):
    inv_freq = ROPE_THETA ** (-jnp.arange(0, ROPE_DIM, 2, dtype=F32) / ROPE_DIM)
    ang = jnp.arange(S, dtype=F32)[:, None] * inv_freq[None, :]
    cos, sin = jnp.cos(ang), jnp.sin(ang)
    return jnp.concatenate([cos, cos], axis=-1), jnp.concatenate([-sin, sin], axis=-1)


def _mixer(x, B, S, tables, lw):
    qkv, gam, sz, ub, vb, qc, kc, vc, gate = _inproj(x, S, tables[0], tables[1], lw)
    q, k, v = _conv_prep(qkv.reshape(B, S, QKV_A_COLS), lw["conv_w"])
    nch = S // DELTA_CHUNK
    gam_rows = jnp.swapaxes(gam[:, :2 * H_A].reshape(B, nch, DELTA_CHUNK, 2 * H_A), 2, 3)
    o_f, o_b = _delta_rule(q, k, v, gam.reshape(B, S, LANES), gam_rows)
    oc = _attention(qc, kc, vc, B, S)
    T = B * S
    return _merge(x, o_f.reshape(T, -1), o_b.reshape(T, -1), sz, ub, vb, oc.reshape(T, -1), gate, lw)


def _peer(h, lw):
    xn, pos1, bm, nbrow, arow = _peer_topk(h, lw)
    return _peer_dense(h, xn, pos1, bm, nbrow, arow, lw)


def kernel(x_prompt, x_sample, mix_norm, w_in, conv_w, a_log, dt_bias, o_norm, sg_norm, sg_w, sg_b, q_a_norm, w_uq,
           kv_a_norm, w_ukv, q_nope_norm, q_rope_norm, k_nope_norm, k_rope_norm, w_branch, w_out, ffn_norm, peer_wq,
           peer_keys, peer_u, peer_v):
    params = (mix_norm, w_in, conv_w, a_log, dt_bias, o_norm, sg_norm, sg_w, sg_b, q_a_norm, w_uq, kv_a_norm, w_ukv,
              q_nope_norm, q_rope_norm, k_nope_norm, k_rope_norm, w_branch, w_out, ffn_norm, peer_wq, peer_keys,
              peer_u, peer_v)
    depth = w_in.shape[0]
    trunks = [x_prompt, x_sample]
    shapes = [t.shape for t in trunks]
    acts = [t.reshape(-1, D_MODEL) for t in trunks]
    tables = [_rope_tables(s[1]) for s in shapes]
    sizes = [a.shape[0] for a in acts]
    for l in range(depth):
        lw = _layer_weights(l, *params)
        mixed = [_mixer(a, s[0], s[1], t, lw) for a, s, t in zip(acts, shapes, tables)]
        y = _peer(jnp.concatenate(mixed, axis=0), lw)
        acts = [y[:sizes[0]], y[sizes[0]:]]
    return tuple(a.reshape(s) for a, s in zip(acts, shapes))
```

```python
import functools
import math

import jax
import jax.numpy as jnp
from jax import lax
from jax.experimental import pallas as pl
from jax.experimental.pallas import tpu as pltpu

D_MODEL = 1024
NORM_EPS = 1e-6
H_A = 4
DK_A = 128
DV_A = 128
CONV_W = 5
DELTA_CHUNK = 64
QKV_A_COLS = 3 * H_A * DK_A
SG_GROUPS = 4
SG_CHUNK = 128
SG_WIDTH = 512
H_C = 4
Q_LORA = 384
KV_LORA = 256
NOPE_DIM = 128
ROPE_DIM = 64
DV_C = 128
QK_HEAD_DIM = NOPE_DIM + ROPE_DIM
ROPE_THETA = 10000.0
SM_SCALE = QK_HEAD_DIM ** -0.5
_LOG2E = math.log2(math.e)
PEER_HEADS = 8
N_KEYS = 128
PEER_QDIM = 256
PEER_TOPK = 16
N_BRANCH = 3

MXU_DTYPE = jnp.bfloat16
LANES = 128
VMEM_LIMIT = 56 * 1024 * 1024

_IN_GROUPS = (("qkv", QKV_A_COLS), ("ab", LANES), ("z", 512), ("u", SG_WIDTH), ("v", SG_WIDTH),
              ("cq", Q_LORA), ("ckv", KV_LORA), ("kr", LANES), ("gate", N_BRANCH * D_MODEL))
_IN_OFF = {}
_off = 0
for _name, _w in _IN_GROUPS:
    _IN_OFF[_name] = (_off, _off + _w)
    _off += _w
IN_COLS_PAD = _off

F32 = jnp.float32


def _mm(a, b):
    return jnp.dot(a.astype(MXU_DTYPE), b.astype(MXU_DTYPE), preferred_element_type=F32)


def _mm_nt(a, b):
    return lax.dot_general(a.astype(MXU_DTYPE), b.astype(MXU_DTYPE), (((1,), (1,)), ((), ())),
                           preferred_element_type=F32)


def _mm_tn(a, b):
    return lax.dot_general(a.astype(MXU_DTYPE), b.astype(MXU_DTYPE), (((0,), (0,)), ((), ())),
                           preferred_element_type=F32)


def _rms(x, gain):
    return x * lax.rsqrt(jnp.mean(x * x, axis=-1, keepdims=True) + NORM_EPS) * gain


def _sigmoid(x):
    return 1.0 / (1.0 + jnp.exp(-x))


def _silu(x):
    return x * _sigmoid(x)


def _gelu(x):
    return x * (0.5 * (1.0 + jnp.tanh(math.sqrt(2.0 / math.pi) * (x + 0.044715 * (x * x * x)))))


def _softplus(x):
    return jnp.maximum(x, 0.0) + jnp.log1p(jnp.exp(-jnp.abs(x)))


def _cparams(sem):
    return pltpu.CompilerParams(dimension_semantics=sem, vmem_limit_bytes=VMEM_LIMIT)


def _vmem_full():
    return pl.BlockSpec(memory_space=pltpu.VMEM)


def _inproj_kernel(x_ref, cos_ref, sin_ref, mixg_ref, win_ref, alog_ref, dtb_ref, sgn_ref, qan_ref, wuq_ref,
                   kvan_ref, wukv_ref, qnn_ref, qrn_ref, knn_ref, krn_ref,
                   qkv_ref, gam_ref, sz_ref, ub_ref, vb_ref, qc_ref, kc_ref, vc_ref, gate_ref):
    tm = x_ref.shape[0]
    xn = _rms(x_ref[...], mixg_ref[...]).astype(MXU_DTYPE)

    def proj(name):
        a, b = _IN_OFF[name]
        return jnp.dot(xn, win_ref[:, a:b], preferred_element_type=F32)

    qkv_ref[...] = proj("qkv")
    sz_ref[...] = _silu(proj("z"))
    ub_ref[...] = _gelu(proj("u"))
    vb_ref[...] = _rms(_gelu(proj("v")), sgn_ref[...]).astype(vb_ref.dtype)
    gate_ref[...] = _sigmoid(proj("gate"))

    ab = proj("ab")
    col = lax.broadcasted_iota(jnp.int32, ab.shape, 1)
    la = jnp.where(col < 2 * H_A, -jnp.exp(alog_ref[...]) * _softplus(ab + dtb_ref[...]), 0.0)
    beta = _sigmoid(ab)
    r = lax.broadcasted_iota(jnp.int32, (tm, tm), 0)
    c = lax.broadcasted_iota(jnp.int32, (tm, tm), 1)
    same = (r // DELTA_CHUNK) == (c // DELTA_CHUNK)
    tri_f = jnp.where(same & (r >= c), 1.0, 0.0).astype(MXU_DTYPE)
    tri_b = jnp.where(same & (r <= c), 1.0, 0.0).astype(MXU_DTYPE)
    p0 = la.astype(MXU_DTYPE)
    r1 = la - p0.astype(F32)
    p1 = r1.astype(MXU_DTYPE)
    p2 = (r1 - p1.astype(F32)).astype(MXU_DTYPE)
    gf = (jnp.dot(tri_f, p0, preferred_element_type=F32) + jnp.dot(tri_f, p1, preferred_element_type=F32)
          + jnp.dot(tri_f, p2, preferred_element_type=F32))
    gb = (jnp.dot(tri_b, p0, preferred_element_type=F32) + jnp.dot(tri_b, p1, preferred_element_type=F32)
          + jnp.dot(tri_b, p2, preferred_element_type=F32))
    gam = jnp.where(col < H_A, gf, gb)
    gam_ref[...] = jnp.where(col < 2 * H_A, gam, jnp.where(col < 4 * H_A, beta, 0.0))

    cos2 = cos_ref[...]
    sin2 = sin_ref[...]

    def rope(t):
        swapped = jnp.concatenate([t[:, ROPE_DIM // 2:], t[:, :ROPE_DIM // 2]], axis=-1)
        return t * cos2 + swapped * sin2

    q = _mm(_rms(proj("cq"), qan_ref[...]), wuq_ref[...])
    kv = _mm(_rms(proj("ckv"), kvan_ref[...]), wukv_ref[...])
    kr = rope(_rms(proj("kr")[:, :ROPE_DIM], krn_ref[...]))
    for h in range(H_C):
        qn = _rms(q[:, h * NOPE_DIM:(h + 1) * NOPE_DIM], qnn_ref[...])
        o = H_C * NOPE_DIM + h * ROPE_DIM
        qr = rope(_rms(q[:, o:o + ROPE_DIM], qrn_ref[...]))
        qc_ref[h] = (jnp.concatenate([qn, qr], axis=-1) * (SM_SCALE * _LOG2E)).astype(qc_ref.dtype)
        kn = _rms(kv[:, h * 256:h * 256 + NOPE_DIM], knn_ref[...])
        kc_ref[h] = jnp.concatenate([kn, kr], axis=-1).astype(kc_ref.dtype)
        vc_ref[h] = kv[:, h * 256 + NOPE_DIM:(h + 1) * 256].astype(vc_ref.dtype)


def _inproj(x, seq, cos2, sin2, lw):
    T = x.shape[0]
    tm = min(256, seq)
    nseq = seq // tm
    row = lambda i: (i, 0)
    full2 = lambda shape: pl.BlockSpec(shape, lambda i: (0, 0))
    in_specs = [
        pl.BlockSpec((tm, D_MODEL), row),
        pl.BlockSpec((tm, ROPE_DIM), lambda i: (i % nseq, 0)),
        pl.BlockSpec((tm, ROPE_DIM), lambda i: (i % nseq, 0)),
        full2((1, D_MODEL)),
        _vmem_full(),
        full2((1, LANES)), full2((1, LANES)), full2((1, SG_WIDTH)), full2((1, Q_LORA)),
        _vmem_full(),
        full2((1, KV_LORA)),
        _vmem_full(),
        full2((1, NOPE_DIM)), full2((1, ROPE_DIM)), full2((1, NOPE_DIM)), full2((1, ROPE_DIM)),
    ]
    head3 = lambda w: pl.BlockSpec((H_C, tm, w), lambda i: (0, i, 0))
    out_shape = (
        jax.ShapeDtypeStruct((T, QKV_A_COLS), F32),
        jax.ShapeDtypeStruct((T, LANES), F32),
        jax.ShapeDtypeStruct((T, 512), F32),
        jax.ShapeDtypeStruct((T, SG_WIDTH), F32),
        jax.ShapeDtypeStruct((T, SG_WIDTH), MXU_DTYPE),
        jax.ShapeDtypeStruct((H_C, T, QK_HEAD_DIM), MXU_DTYPE),
        jax.ShapeDtypeStruct((H_C, T, QK_HEAD_DIM), MXU_DTYPE),
        jax.ShapeDtypeStruct((H_C, T, DV_C), MXU_DTYPE),
        jax.ShapeDtypeStruct((T, N_BRANCH * D_MODEL), F32),
    )
    out_specs = (
        pl.BlockSpec((tm, QKV_A_COLS), row), pl.BlockSpec((tm, LANES), row), pl.BlockSpec((tm, 512), row),
        pl.BlockSpec((tm, SG_WIDTH), row), pl.BlockSpec((tm, SG_WIDTH), row),
        head3(QK_HEAD_DIM), head3(QK_HEAD_DIM), head3(DV_C),
        pl.BlockSpec((tm, N_BRANCH * D_MODEL), row),
    )
    return pl.pallas_call(
        _inproj_kernel, name="inproj", grid=(T // tm,), in_specs=in_specs, out_specs=out_specs, out_shape=out_shape,
        compiler_params=_cparams(("parallel",)),
    )(x, cos2, sin2, lw["mix_norm"], lw["w_in"], lw["a_log"], lw["dt_bias"], lw["sg_norm"], lw["q_a_norm"],
      lw["w_uq"], lw["kv_a_norm"], lw["w_ukv"], lw["q_nope_norm"], lw["q_rope_norm"], lw["k_nope_norm"],
      lw["k_rope_norm"])


_HALO = 8


def _conv_kernel(prev_ref, x_ref, next_ref, w_ref, q_ref, k_ref, v_ref, ext_ref):
    i = pl.program_id(1)
    n = pl.num_programs(1)
    ts = x_ref.shape[1]
    ext_ref[0:_HALO, :] = jnp.where(i > 0, prev_ref[0], 0.0)
    ext_ref[_HALO:_HALO + ts, :] = x_ref[0]
    ext_ref[_HALO + ts:, :] = jnp.where(i < n - 1, next_ref[0], 0.0)
    acc = None
    for j in range(CONV_W):
        start = _HALO - CONV_W // 2 + j
        term = ext_ref[start:start + ts, :] * w_ref[j:j + 1, :]
        acc = term if acc is None else acc + term
    y = _silu(acc)
    for h in range(H_A):
        qh = y[:, h * DK_A:(h + 1) * DK_A]
        kh = y[:, (H_A + h) * DK_A:(H_A + h + 1) * DK_A]
        q_ref[0, :, h * DK_A:(h + 1) * DK_A] = qh * (
            lax.rsqrt(jnp.sum(qh * qh, axis=-1, keepdims=True) + NORM_EPS) * (DK_A ** -0.5))
        k_ref[0, :, h * DK_A:(h + 1) * DK_A] = kh * lax.rsqrt(jnp.sum(kh * kh, axis=-1, keepdims=True) + NORM_EPS)
    v_ref[0] = y[:, 2 * H_A * DK_A:]


def _conv_prep(qkv, conv_w):
    B, S, C = qkv.shape
    ts = min(512, S)
    nb = S // ts
    hb = ts // _HALO
    nh = S // _HALO
    out = jax.ShapeDtypeStruct((B, S, H_A * DK_A), F32)
    blk = pl.BlockSpec((1, ts, H_A * DK_A), lambda b, i: (b, i, 0))
    return pl.pallas_call(
        _conv_kernel, name="conv_prep", grid=(B, nb),
        in_specs=[
            pl.BlockSpec((1, _HALO, C), lambda b, i: (b, jnp.maximum(i * hb - 1, 0), 0)),
            pl.BlockSpec((1, ts, C), lambda b, i: (b, i, 0)),
            pl.BlockSpec((1, _HALO, C), lambda b, i: (b, jnp.minimum((i + 1) * hb, nh - 1), 0)),
            pl.BlockSpec((CONV_W, C), lambda b, i: (0, 0)),
        ],
        out_specs=(blk, blk, blk), out_shape=(out, out, out),
        scratch_shapes=[pltpu.VMEM((ts + 2 * _HALO, C), F32)],
        compiler_params=_cparams(("parallel", "parallel")),
    )(qkv, qkv, qkv, conv_w)


def _delta_units(q, k, v, gam_c, gam_r, b_c, state, backward):
    C = DELTA_CHUNK
    U = range(len(q))
    r = lax.broadcasted_iota(jnp.int32, (C, C), 0)
    c = lax.broadcasted_iota(jnp.int32, (C, C), 1)
    incl = [r <= c if backward[u] else r >= c for u in U]
    strict = [r < c if backward[u] else r > c for u in U]
    last = [0 if backward[u] else C - 1 for u in U]
    decay = [jnp.where(incl[u], jnp.exp(jnp.where(incl[u], gam_c[u] - gam_r[u], 0.0)), 0.0) for u in U]
    kb = [k[u] * b_c[u] for u in U]
    a = [jnp.where(strict[u], _mm_nt(kb[u], k[u]) * decay[u], 0.0) for u in U]
    qk = [_mm_nt(q[u], k[u]) * decay[u] for u in U]
    eg = [jnp.exp(gam_c[u]) for u in U]
    y = [jnp.concatenate([v[u] * b_c[u], kb[u] * eg[u]], axis=-1) for u in U]
    y = [y[u] - _mm(a[u], y[u]) for u in U]
    p = a
    for _ in range(5):
        p = [_mm(p[u], p[u]) for u in U]
        y = [y[u] + _mm(p[u], y[u]) for u in U]
    g_last = [gam_c[u][last[u]:last[u] + 1, :] for u in U]
    k_dec = [k[u] * jnp.exp(g_last[u] - gam_c[u]) for u in U]
    q_dec = [q[u] * eg[u] for u in U]
    v_new = [y[u][:, :DV_A] - _mm(y[u][:, DV_A:], state[u]) for u in U]
    o_state = [_mm(q_dec[u], state[u]) for u in U]
    o = [o_state[u] + _mm(qk[u], v_new[u]) for u in U]
    new_state = [state[u] * jnp.exp(g_last[u]) + _mm_tn(k_dec[u], v_new[u]) for u in U]
    return o, new_state


def _delta_kernel(qf_ref, kf_ref, vf_ref, gf_ref, rf_ref, qb_ref, kb_ref, vb_ref, gb_ref, rb_ref,
                  of_ref, ob_ref, state_ref):
    C = DELTA_CHUNK
    nc = qf_ref.shape[1] // C

    @pl.when(pl.program_id(1) == 0)
    def _():
        state_ref[...] = jnp.zeros_like(state_ref)

    def body(ci, carry):
        args = [[] for _ in range(8)]
        dests = []
        for d, (q_ref, k_ref, v_ref, g_ref, r_ref, o_ref) in enumerate(
                ((qf_ref, kf_ref, vf_ref, gf_ref, rf_ref, of_ref), (qb_ref, kb_ref, vb_ref, gb_ref, rb_ref, ob_ref))):
            cc = ci if d == 0 else nc - 1 - ci
            rows = pl.ds(pl.multiple_of(cc * C, C), C)
            g = g_ref[0, rows, :]
            grow = r_ref[0, cc]
            for h in range(H_A):
                lanes = slice(h * DK_A, (h + 1) * DK_A)
                j = d * H_A + h
                unit = (q_ref[0, rows, lanes], k_ref[0, rows, lanes], v_ref[0, rows, lanes], g[:, j:j + 1],
                        grow[j:j + 1, :], g[:, 2 * H_A + j:2 * H_A + j + 1], state_ref[j], d == 1)
                for lst, val in zip(args, unit):
                    lst.append(val)
                dests.append((o_ref, rows, lanes, j))
        outs, states = _delta_units(*args)
        for (o_ref, rows, lanes, j), o, s_new in zip(dests, outs, states):
            o_ref[0, rows, lanes] = o
            state_ref[j] = s_new
        return carry

    lax.fori_loop(0, nc, body, 0)


def _delta_rule(q, k, v, gam, gam_rows):
    B, S, W = q.shape
    ts = min(256, S)
    nb = S // ts
    ncb = ts // DELTA_CHUNK
    fwd = lambda b, i: (b, i, 0)
    bwd = lambda b, i: (b, nb - 1 - i, 0)
    fwd4 = lambda b, i: (b, i, 0, 0)
    bwd4 = lambda b, i: (b, nb - 1 - i, 0, 0)

    def specs(m3, m4):
        return [pl.BlockSpec((1, ts, W), m3), pl.BlockSpec((1, ts, W), m3), pl.BlockSpec((1, ts, W), m3),
                pl.BlockSpec((1, ts, LANES), m3), pl.BlockSpec((1, ncb, 2 * H_A, DELTA_CHUNK), m4)]

    out = jax.ShapeDtypeStruct((B, S, W), F32)
    return pl.pallas_call(
        _delta_kernel, name="delta_rule", grid=(B, nb), in_specs=specs(fwd, fwd4) + specs(bwd, bwd4),
        out_specs=(pl.BlockSpec((1, ts, W), fwd), pl.BlockSpec((1, ts, W), bwd)), out_shape=(out, out),
        scratch_shapes=[pltpu.VMEM((2 * H_A, DK_A, DV_A), F32)],
        compiler_params=_cparams(("parallel", "arbitrary")),
    )(q, k, v, gam, gam_rows, q, k, v, gam, gam_rows)


def _attn_kernel(q_ref, k_ref, v_ref, o_ref, m_ref, l_ref, acc_ref):
    kv = pl.program_id(3)

    @pl.when(kv == 0)
    def _():
        m_ref[...] = jnp.full_like(m_ref, -jnp.inf)
        l_ref[...] = jnp.zeros_like(l_ref)
        acc_ref[...] = jnp.zeros_like(acc_ref)

    q = q_ref[0, 0]
    nk = k_ref.shape[2] // _ATTN_KSUB
    nt = _ATTN_KSUB // LANES
    ksub = lambda j: slice(j * _ATTN_KSUB, (j + 1) * _ATTN_KSUB)
    m = m_ref[...]
    l = l_ref[...]
    acc = acc_ref[...]
    s = _mm_nt(q, k_ref[0, 0, ksub(0), :])
    for j in range(nk):
        s_next = _mm_nt(q, k_ref[0, 0, ksub(j + 1), :]) if j + 1 < nk else None
        tiles = [s[:, t * LANES:(t + 1) * LANES] for t in range(nt)]
        m_new = jnp.maximum(m, jnp.max(functools.reduce(jnp.maximum, tiles), axis=-1, keepdims=True))
        alpha = jnp.exp2(m - m_new)
        ps = [jnp.exp2(t - m_new) for t in tiles]
        l = alpha * l + functools.reduce(lambda a, b: a + b, ps)
        p = jnp.concatenate([t.astype(MXU_DTYPE) for t in ps], axis=-1)
        acc = alpha * acc + jnp.dot(p, v_ref[0, 0, ksub(j), :], preferred_element_type=F32)
        m = m_new
        s = s_next
    m_ref[...] = m
    l_ref[...] = l
    acc_ref[...] = acc

    @pl.when(kv == pl.num_programs(3) - 1)
    def _():
        o_ref[0] = (acc / jnp.sum(l, axis=-1, keepdims=True)).astype(o_ref.dtype)


_ATTN_KSUB = 512


def _attention(qc, kc, vc, B, S):
    tq = min(512, S)
    tk = min(2048, S)
    qc = qc.reshape(H_C, B, S, QK_HEAD_DIM)
    kc = kc.reshape(H_C, B, S, QK_HEAD_DIM)
    vc = vc.reshape(H_C, B, S, DV_C)
    return pl.pallas_call(
        _attn_kernel, name="attention", grid=(B, H_C, S // tq, S // tk),
        in_specs=[pl.BlockSpec((1, 1, tq, QK_HEAD_DIM), lambda b, h, i, j: (h, b, i, 0)),
                  pl.BlockSpec((1, 1, tk, QK_HEAD_DIM), lambda b, h, i, j: (h, b, j, 0)),
                  pl.BlockSpec((1, 1, tk, DV_C), lambda b, h, i, j: (h, b, j, 0))],
        out_specs=pl.BlockSpec((1, tq, DV_C), lambda b, h, i, j: (b, i, h)),
        out_shape=jax.ShapeDtypeStruct((B, S, H_C * DV_C), MXU_DTYPE),
        scratch_shapes=[pltpu.VMEM((tq, LANES), F32), pltpu.VMEM((tq, LANES), F32), pltpu.VMEM((tq, DV_C), F32)],
        compiler_params=_cparams(("parallel", "parallel", "parallel", "arbitrary")),
    )(qc, kc, vc)


def _merge_kernel(x_ref, of_ref, ob_ref, sz_ref, ub_ref, vb_ref, oc_ref, gate_ref, onorm_ref, sgw_ref, sgb_ref,
                  wbr_ref, wout_ref, h_ref):
    tm = x_ref.shape[0]
    o = of_ref[...] + ob_ref[...]
    sz = sz_ref[...]
    out_a = jnp.concatenate(
        [_rms(o[:, h * DV_A:(h + 1) * DV_A], onorm_ref[...]) * sz[:, h * DV_A:(h + 1) * DV_A] for h in range(H_A)],
        axis=-1)
    gw = SG_WIDTH // SG_GROUPS
    chunks = []
    for ci in range(tm // SG_CHUNK):
        rows = slice(ci * SG_CHUNK, (ci + 1) * SG_CHUNK)
        mixed = jnp.concatenate(
            [jnp.dot(sgw_ref[g], vb_ref[rows, g * gw:(g + 1) * gw], preferred_element_type=F32) + sgb_ref[:, g:g + 1]
             for g in range(SG_GROUPS)], axis=-1)
        chunks.append(ub_ref[rows, :] * mixed)
    out_b = jnp.concatenate(chunks, axis=0)
    gate = gate_ref[...]
    merged = (gate[:, :D_MODEL] * _mm(out_a, wbr_ref[0])
              + gate[:, D_MODEL:2 * D_MODEL] * _mm(out_b, wbr_ref[1])
              + gate[:, 2 * D_MODEL:] * jnp.dot(oc_ref[...], wbr_ref[2], preferred_element_type=F32))
    h_ref[...] = x_ref[...] + _mm(merged, wout_ref[...])


def _merge(x, o_f, o_b, sz, ub, vb, oc, gate, lw):
    T = x.shape[0]
    tm = min(256, T)
    row = lambda i: (i, 0)
    w512 = pl.BlockSpec((tm, 512), row)
    return pl.pallas_call(
        _merge_kernel, name="merge", grid=(T // tm,),
        in_specs=[pl.BlockSpec((tm, D_MODEL), row), w512, w512, w512, w512, w512, w512,
                  pl.BlockSpec((tm, N_BRANCH * D_MODEL), row),
                  pl.BlockSpec((1, DV_A), lambda i: (0, 0)),
                  _vmem_full(), _vmem_full(), _vmem_full(), _vmem_full()],
        out_specs=pl.BlockSpec((tm, D_MODEL), row),
        out_shape=jax.ShapeDtypeStruct((T, D_MODEL), F32),
        compiler_params=_cparams(("parallel",)),
    )(x, o_f, o_b, sz, ub, vb, oc, gate, lw["o_norm"], lw["sg_w"], lw["sg_bT"], lw["w_branch"], lw["w_out"])


_NEG_INF = float("-inf")
_CAND_B_LIMIT = tuple(PEER_TOPK // (a + 1) for a in range(PEER_TOPK))


GATE_DTYPE = jnp.bfloat16


def _pack_gate_pair(lo, hi):
    lo_bits = lax.bitcast_convert_type(lo.astype(GATE_DTYPE).astype(F32), jnp.uint32)
    hi_bits = lax.bitcast_convert_type(hi.astype(GATE_DTYPE).astype(F32), jnp.uint32)
    return (lo_bits >> 16) | (hi_bits & jnp.uint32(0xFFFF0000))


def _top16_rows(s, top_ref):
    n, tt = s.shape
    rid = lax.broadcasted_iota(jnp.int32, (n, tt), 0)
    work = s
    pos = jnp.full((n, tt), float(PEER_TOPK), F32)
    for r in range(PEER_TOPK):
        m = jnp.max(work, axis=0, keepdims=True)
        idx = jnp.min(jnp.where(work == m, rid, n), axis=0, keepdims=True)
        sel = rid == idx
        pos = jnp.where(sel, float(r), pos)
        work = jnp.where(sel, _NEG_INF, work)
        top_ref[r:r + 1, :] = m
    return pos


def _peer_topk_kernel(h_ref, gain_ref, wq_ref, keys_ref, xn_ref, pos1_ref, bm_ref, nbrow_ref, arow_ref,
                      q_ref, t0_ref, t1_ref):
    tt = h_ref.shape[0]
    xn = _rms(h_ref[...], gain_ref[...]).astype(MXU_DTYPE)
    xn_ref[...] = xn
    q_ref[...] = jnp.dot(xn, wq_ref[...], preferred_element_type=F32).astype(MXU_DTYPE)
    sub = lax.broadcasted_iota(jnp.int32, (8, tt), 0)

    def head(h, carry):
        base = pl.multiple_of(h * PEER_QDIM, PEER_QDIM)
        s0 = _mm_nt(keys_ref[h, 0], q_ref[:, pl.ds(base, N_KEYS)])
        s1 = _mm_nt(keys_ref[h, 1], q_ref[:, pl.ds(base + N_KEYS, N_KEYS)])
        pos0 = _top16_rows(s0, t0_ref)
        pos1 = _top16_rows(s1, t1_ref)
        t0 = t0_ref[...]
        t1 = t1_ref[...]
        t00 = t0[0:1, :]
        t10 = t1[0:1, :]
        c00 = t00 + t10
        vals = [t00 + t1[0:8, :], t00 + t1[8:16, :]]
        cidx = [sub, sub + 8]
        for a in range(1, 8):
            vals.append(jnp.where(sub < _CAND_B_LIMIT[a], t0[a:a + 1, :] + t1[0:8, :], _NEG_INF))
            cidx.append(sub + a * PEER_TOPK)
        vals.append(t0[8:16, :] + t10)
        cidx.append((sub + 8) * PEER_TOPK)
        orig = list(vals)
        big = PEER_TOPK * PEER_TOPK
        for _ in range(PEER_TOPK):
            m = functools.reduce(jnp.maximum, vals)
            m = jnp.max(m, axis=0, keepdims=True)
            cand = functools.reduce(jnp.minimum, [jnp.where(v == m, ci, big) for v, ci in zip(vals, cidx)])
            idx = jnp.min(cand, axis=0, keepdims=True)
            vals = [jnp.where(ci == idx, _NEG_INF, v) for v, ci in zip(vals, cidx)]
        sel = [jnp.where(v != o, 1.0, 0.0) for v, o in zip(vals, orig)]
        z = functools.reduce(
            lambda x, y: x + y,
            [jnp.sum(jnp.where(s > 0.0, jnp.exp(jnp.where(s > 0.0, o - c00, 0.0)), 0.0), axis=0, keepdims=True)
             for s, o in zip(sel, orig)])
        nb_lo = [jnp.sum(sel[0], axis=0, keepdims=True) + jnp.sum(sel[1], axis=0, keepdims=True)]
        for a in range(1, 8):
            nb_lo.append(jnp.sum(sel[a + 1], axis=0, keepdims=True))
        nb_hi = sel[9]
        nbrow = jnp.zeros((N_KEYS, tt), F32)
        for a in range(PEER_TOPK):
            nba = nb_lo[a] if a < 8 else nb_hi[a - 8:a - 7, :]
            nbrow = jnp.where(pos0 == float(a), nba, nbrow)
        inv_z = 1.0 / z
        arow = jnp.where(pos0 < float(PEER_TOPK), jnp.exp(jnp.where(pos0 < float(PEER_TOPK), s0 - t00, 0.0)),
                         0.0) * inv_z
        bm = jnp.where(pos1 < float(PEER_TOPK), jnp.exp(jnp.where(pos1 < float(PEER_TOPK), s1 - t10, 0.0)), 0.0)
        nbrow_ref[h] = _pack_gate_pair(nbrow, nbrow)
        arow_ref[h] = _pack_gate_pair(arow, arow)
        half = N_KEYS // 2
        pos1_ref[h] = _pack_gate_pair(pos1[:half], pos1[half:])
        bm_ref[h] = _pack_gate_pair(bm[:half], bm[half:])
        return carry

    lax.fori_loop(0, PEER_HEADS, head, 0)


def _peer_topk(h, lw):
    T = h.shape[0]
    tt = min(256, T)
    half = N_KEYS // 2
    rows_meta = jax.ShapeDtypeStruct((PEER_HEADS, N_KEYS, T), jnp.uint32)
    pair_meta = jax.ShapeDtypeStruct((PEER_HEADS, half, T), jnp.uint32)
    rows_spec = pl.BlockSpec((PEER_HEADS, N_KEYS, tt), lambda i: (0, 0, i))
    pair_spec = pl.BlockSpec((PEER_HEADS, half, tt), lambda i: (0, 0, i))
    return pl.pallas_call(
        _peer_topk_kernel, name="peer_topk", grid=(T // tt,),
        in_specs=[pl.BlockSpec((tt, D_MODEL), lambda i: (i, 0)), pl.BlockSpec((1, D_MODEL), lambda i: (0, 0)),
                  _vmem_full(), _vmem_full()],
        out_specs=(pl.BlockSpec((tt, D_MODEL), lambda i: (i, 0)), pair_spec, pair_spec, rows_spec, rows_spec),
        out_shape=(jax.ShapeDtypeStruct((T, D_MODEL), MXU_DTYPE), pair_meta, pair_meta, rows_meta, rows_meta),
        scratch_shapes=[pltpu.VMEM((tt, PEER_HEADS * PEER_QDIM), MXU_DTYPE), pltpu.VMEM((PEER_TOPK, tt), F32),
                        pltpu.VMEM((PEER_TOPK, tt), F32)],
        compiler_params=_cparams(("parallel",)),
    )(h, lw["ffn_norm"], lw["peer_wq"], lw["peer_keys"])


_PEER_IB = 8


def _peer_dense_kernel(h_ref, xn_ref, u_ref, vt_ref, pos1_ref, bm_ref, nbrow_ref, arow_ref, o_ref,
                       acc_ref, hid_ref, gh_ref):
    ib = pl.program_id(1)
    tt = xn_ref.shape[0]
    half = N_KEYS // 2

    @pl.when(ib == 0)
    def _():
        acc_ref[...] = jnp.zeros_like(acc_ref)

    hid_ref[...] = _mm_nt(u_ref[...], xn_ref[...])
    row0 = pl.ds(pl.multiple_of(ib * _PEER_IB, _PEER_IB), _PEER_IB)
    unpack = lambda words: pltpu.bitcast(words, GATE_DTYPE)
    for il in range(_PEER_IB):
        rows = slice(il * N_KEYS, (il + 1) * N_KEYS)
        for st in range(tt // LANES):
            lanes = slice(st * LANES, (st + 1) * LANES)
            g = None
            for h in range(PEER_HEADS):
                nb = unpack(jnp.broadcast_to(nbrow_ref[h, row0, lanes][il:il + 1, :], (half, LANES)))
                ar = unpack(jnp.broadcast_to(arow_ref[h, row0, lanes][il:il + 1, :], (half, LANES)))
                term = jnp.where(unpack(pos1_ref[h, :, lanes]) < nb, unpack(bm_ref[h, :, lanes]), 0) * ar
                g = term if g is None else g + term
            gh_ref[rows, lanes] = g * _gelu(hid_ref[rows, lanes]).astype(GATE_DTYPE)
    acc_ref[...] += jnp.dot(vt_ref[...], gh_ref[...].astype(vt_ref.dtype), preferred_element_type=F32)

    @pl.when(ib == pl.num_programs(1) - 1)
    def _():
        o_ref[...] = h_ref[...] + acc_ref[...].T


def _peer_dense(h, xn, pos1, bm, nbrow, arow, lw):
    T = h.shape[0]
    tt = min(512, T)
    eb = _PEER_IB * N_KEYS
    rows_spec = pl.BlockSpec((PEER_HEADS, N_KEYS, tt), lambda i, j: (0, 0, i))
    pair_spec = pl.BlockSpec((PEER_HEADS, N_KEYS // 2, tt), lambda i, j: (0, 0, i))
    return pl.pallas_call(
        _peer_dense_kernel, name="peer_dense", grid=(T // tt, N_KEYS // _PEER_IB),
        in_specs=[pl.BlockSpec((tt, D_MODEL), lambda i, j: (i, 0)), pl.BlockSpec((tt, D_MODEL), lambda i, j: (i, 0)),
                  pl.BlockSpec((eb, D_MODEL), lambda i, j: (j, 0)), pl.BlockSpec((D_MODEL, eb), lambda i, j: (0, j)),
                  pair_spec, pair_spec, rows_spec, rows_spec],
        out_specs=pl.BlockSpec((tt, D_MODEL), lambda i, j: (i, 0)),
        out_shape=jax.ShapeDtypeStruct((T, D_MODEL), F32),
        scratch_shapes=[pltpu.VMEM((D_MODEL, tt), F32), pltpu.VMEM((eb, tt), F32), pltpu.VMEM((eb, tt), GATE_DTYPE)],
        compiler_params=_cparams(("parallel", "arbitrary")),
    )(h, xn, lw["peer_u"], lw["peer_vT"], pos1, bm, nbrow, arow)


def _pad_cols(w, width):
    return jnp.pad(w, ((0, 0), (0, width - w.shape[1])))


def _layer_weights(l, mix_norm, w_in, conv_w, a_log, dt_bias, o_norm, sg_norm, sg_w, sg_b, q_a_norm, w_uq, kv_a_norm,
                   w_ukv, q_nope_norm, q_rope_norm, k_nope_norm, k_rope_norm, w_branch, w_out, ffn_norm, peer_wq,
                   peer_keys, peer_u, peer_v):
    split = [QKV_A_COLS, 2 * H_A, 2 * H_A, H_A * DV_A, SG_WIDTH, SG_WIDTH, Q_LORA, KV_LORA, ROPE_DIM,
             N_BRANCH * D_MODEL]
    offs = [0]
    for s in split:
        offs.append(offs[-1] + s)
    wl = w_in[l]
    piece = lambda i: wl[:, offs[i]:offs[i + 1]]
    w_cat = jnp.concatenate([
        piece(0), _pad_cols(jnp.concatenate([piece(1), piece(2)], axis=1), LANES), piece(3), piece(4), piece(5),
        piece(6), piece(7), _pad_cols(piece(8), LANES), piece(9)], axis=1).astype(MXU_DTYPE)
    wq3 = w_uq[l].reshape(Q_LORA, H_C, QK_HEAD_DIM)
    w_uq_r = jnp.concatenate([wq3[:, :, :NOPE_DIM].reshape(Q_LORA, H_C * NOPE_DIM),
                              wq3[:, :, NOPE_DIM:].reshape(Q_LORA, H_C * ROPE_DIM)], axis=1).astype(MXU_DTYPE)
    row = lambda v: v.reshape(1, -1).astype(F32)
    return {
        "mix_norm": row(mix_norm[l]), "w_in": w_cat, "conv_w": conv_w[l].astype(F32),
        "a_log": _pad_cols(row(a_log[l]), LANES), "dt_bias": _pad_cols(row(dt_bias[l]), LANES),
        "o_norm": row(o_norm[l]), "sg_norm": row(sg_norm[l]), "sg_w": sg_w[l].astype(MXU_DTYPE),
        "sg_bT": sg_b[l].T.astype(F32), "q_a_norm": row(q_a_norm[l]), "w_uq": w_uq_r,
        "kv_a_norm": row(kv_a_norm[l]), "w_ukv": w_ukv[l].astype(MXU_DTYPE), "q_nope_norm": row(q_nope_norm[l]),
        "q_rope_norm": row(q_rope_norm[l]), "k_nope_norm": row(k_nope_norm[l]), "k_rope_norm": row(k_rope_norm[l]),
        "w_branch": w_branch[l].astype(MXU_DTYPE), "w_out": w_out[l].astype(MXU_DTYPE), "ffn_norm": row(ffn_norm[l]),
        "peer_wq": peer_wq[l].astype(MXU_DTYPE), "peer_keys": peer_keys[l].astype(MXU_DTYPE),
        "peer_u": _pair_order(peer_u[l].astype(MXU_DTYPE)), "peer_vT": _pair_order(peer_v[l].astype(MXU_DTYPE)).T,
    }


def _pair_order(table):
    half = N_KEYS // 2
    return table.reshape(N_KEYS, 2, half, D_MODEL).swapaxes(1, 2).reshape(N_KEYS * N_KEYS, D_MODEL)


def _rope_tables(S):
    inv_freq = ROPE_THETA ** (-jnp.arange(0, ROPE_DIM, 2, dtype=F32) / ROPE_DIM)
    ang = jnp.arange(S, dtype=F32)[:, None] * inv_freq[None, :]
    cos, sin = jnp.cos(ang), jnp.sin(ang)
    return jnp.concatenate([cos, cos], axis=-1), jnp.concatenate([-sin, sin], axis=-1)


def _mixer(x, B, S, tables, lw):
    qkv, gam, sz, ub, vb, qc, kc, vc, gate = _inproj(x, S, tables[0], tables[1], lw)
    q, k, v = _conv_prep(qkv.reshape(B, S, QKV_A_COLS), lw["conv_w"])
    nch = S // DELTA_CHUNK
    gam_rows = jnp.swapaxes(gam[:, :2 * H_A].reshape(B, nch, DELTA_CHUNK, 2 * H_A), 2, 3)
    o_f, o_b = _delta_rule(q, k, v, gam.reshape(B, S, LANES), gam_rows)
    oc = _attention(qc, kc, vc, B, S)
    T = B * S
    return _merge(x, o_f.reshape(T, -1), o_b.reshape(T, -1), sz, ub, vb, oc.reshape(T, -1), gate, lw)


def _peer(h, lw):
    xn, pos1, bm, nbrow, arow = _peer_topk(h, lw)
    return _peer_dense(h, xn, pos1, bm, nbrow, arow, lw)


def kernel(x_prompt, x_sample, mix_norm, w_in, conv_w, a_log, dt_bias, o_norm, sg_norm, sg_w, sg_b, q_a_norm, w_uq,
           kv_a_norm, w_ukv, q_nope_norm, q_rope_norm, k_nope_norm, k_rope_norm, w_branch, w_out, ffn_norm, peer_wq,
           peer_keys, peer_u, peer_v):
    params = (mix_norm, w_in, conv_w, a_log, dt_bias, o_norm, sg_norm, sg_w, sg_b, q_a_norm, w_uq, kv_a_norm, w_ukv,
              q_nope_norm, q_rope_norm, k_nope_norm, k_rope_norm, w_branch, w_out, ffn_norm, peer_wq, peer_keys,
              peer_u, peer_v)
    depth = w_in.shape[0]
    trunks = [x_prompt, x_sample]
    shapes = [t.shape for t in trunks]
    acts = [t.reshape(-1, D_MODEL) for t in trunks]
    tables = [_rope_tables(s[1]) for s in shapes]
    sizes = [a.shape[0] for a in acts]
    for l in range(depth):
        lw = _layer_weights(l, *params)
        mixed = [_mixer(a, s[0], s[1], t, lw) for a, s, t in zip(acts, shapes, tables)]
        y = _peer(jnp.concatenate(mixed, axis=0), lw)
        acts = [y[:sizes[0]], y[sizes[0]:]]
    return tuple(a.reshape(s) for a, s in zip(acts, shapes))
```

```python
import functools
import math

import jax
import jax.numpy as jnp
from jax import lax
from jax.experimental import pallas as pl
from jax.experimental.pallas import tpu as pltpu

D_MODEL = 1024
NORM_EPS = 1e-6
H_A = 4
DK_A = 128
DV_A = 128
CONV_W = 5
DELTA_CHUNK = 64
QKV_A_COLS = 3 * H_A * DK_A
SG_GROUPS = 4
SG_CHUNK = 128
SG_WIDTH = 512
H_C = 4
Q_LORA = 384
KV_LORA = 256
NOPE_DIM = 128
ROPE_DIM = 64
DV_C = 128
QK_HEAD_DIM = NOPE_DIM + ROPE_DIM
ROPE_THETA = 10000.0
SM_SCALE = QK_HEAD_DIM ** -0.5
_LOG2E = math.log2(math.e)
PEER_HEADS = 8
N_KEYS = 128
PEER_QDIM = 256
PEER_TOPK = 16
N_BRANCH = 3

MXU_DTYPE = jnp.bfloat16
LANES = 128
VMEM_LIMIT = 56 * 1024 * 1024

_IN_GROUPS = (("qkv", QKV_A_COLS), ("ab", LANES), ("z", 512), ("u", SG_WIDTH), ("v", SG_WIDTH),
              ("cq", Q_LORA), ("ckv", KV_LORA), ("kr", LANES), ("gate", N_BRANCH * D_MODEL))
_IN_OFF = {}
_off = 0
for _name, _w in _IN_GROUPS:
    _IN_OFF[_name] = (_off, _off + _w)
    _off += _w
IN_COLS_PAD = _off

F32 = jnp.float32


def _mm(a, b):
    return jnp.dot(a.astype(MXU_DTYPE), b.astype(MXU_DTYPE), preferred_element_type=F32)


def _mm_nt(a, b):
    return lax.dot_general(a.astype(MXU_DTYPE), b.astype(MXU_DTYPE), (((1,), (1,)), ((), ())),
                           preferred_element_type=F32)


def _mm_tn(a, b):
    return lax.dot_general(a.astype(MXU_DTYPE), b.astype(MXU_DTYPE), (((0,), (0,)), ((), ())),
                           preferred_element_type=F32)


def _rms(x, gain):
    return x * lax.rsqrt(jnp.mean(x * x, axis=-1, keepdims=True) + NORM_EPS) * gain


def _sigmoid(x):
    return 1.0 / (1.0 + jnp.exp(-x))


def _silu(x):
    return x * _sigmoid(x)


def _gelu(x):
    return x * (0.5 * (1.0 + jnp.tanh(math.sqrt(2.0 / math.pi) * (x + 0.044715 * (x * x * x)))))


def _softplus(x):
    return jnp.maximum(x, 0.0) + jnp.log1p(jnp.exp(-jnp.abs(x)))


def _cparams(sem):
    return pltpu.CompilerParams(dimension_semantics=sem, vmem_limit_bytes=VMEM_LIMIT)


def _vmem_full():
    return pl.BlockSpec(memory_space=pltpu.VMEM)


def _inproj_kernel(x_ref, cos_ref, sin_ref, mixg_ref, win_ref, alog_ref, dtb_ref, sgn_ref, qan_ref, wuq_ref,
                   kvan_ref, wukv_ref, qnn_ref, qrn_ref, knn_ref, krn_ref,
                   qkv_ref, gam_ref, sz_ref, ub_ref, vb_ref, qc_ref, kc_ref, vc_ref, gate_ref):
    tm = x_ref.shape[0]
    xn = _rms(x_ref[...], mixg_ref[...]).astype(MXU_DTYPE)

    def proj(name):
        a, b = _IN_OFF[name]
        return jnp.dot(xn, win_ref[:, a:b], preferred_element_type=F32)

    qkv_ref[...] = proj("qkv")
    sz_ref[...] = _silu(proj("z"))
    ub_ref[...] = _gelu(proj("u"))
    vb_ref[...] = _rms(_gelu(proj("v")), sgn_ref[...]).astype(vb_ref.dtype)
    gate_ref[...] = _sigmoid(proj("gate"))

    ab = proj("ab")
    col = lax.broadcasted_iota(jnp.int32, ab.shape, 1)
    la = jnp.where(col < 2 * H_A, -jnp.exp(alog_ref[...]) * _softplus(ab + dtb_ref[...]), 0.0)
    beta = _sigmoid(ab)
    r = lax.broadcasted_iota(jnp.int32, (tm, tm), 0)
    c = lax.broadcasted_iota(jnp.int32, (tm, tm), 1)
    same = (r // DELTA_CHUNK) == (c // DELTA_CHUNK)
    tri_f = jnp.where(same & (r >= c), 1.0, 0.0).astype(MXU_DTYPE)
    tri_b = jnp.where(same & (r <= c), 1.0, 0.0).astype(MXU_DTYPE)
    p0 = la.astype(MXU_DTYPE)
    r1 = la - p0.astype(F32)
    p1 = r1.astype(MXU_DTYPE)
    p2 = (r1 - p1.astype(F32)).astype(MXU_DTYPE)
    gf = (jnp.dot(tri_f, p0, preferred_element_type=F32) + jnp.dot(tri_f, p1, preferred_element_type=F32)
          + jnp.dot(tri_f, p2, preferred_element_type=F32))
    gb = (jnp.dot(tri_b, p0, preferred_element_type=F32) + jnp.dot(tri_b, p1, preferred_element_type=F32)
          + jnp.dot(tri_b, p2, preferred_element_type=F32))
    gam = jnp.where(col < H_A, gf, gb)
    gam_ref[...] = jnp.where(col < 2 * H_A, gam, jnp.where(col < 4 * H_A, beta, 0.0))

    cos2 = cos_ref[...]
    sin2 = sin_ref[...]

    def rope(t):
        swapped = jnp.concatenate([t[:, ROPE_DIM // 2:], t[:, :ROPE_DIM // 2]], axis=-1)
        return t * cos2 + swapped * sin2

    q = _mm(_rms(proj("cq"), qan_ref[...]), wuq_ref[...])
    kv = _mm(_rms(proj("ckv"), kvan_ref[...]), wukv_ref[...])
    kr = rope(_rms(proj("kr")[:, :ROPE_DIM], krn_ref[...]))
    for h in range(H_C):
        qn = _rms(q[:, h * NOPE_DIM:(h + 1) * NOPE_DIM], qnn_ref[...])
        o = H_C * NOPE_DIM + h * ROPE_DIM
        qr = rope(_rms(q[:, o:o + ROPE_DIM], qrn_ref[...]))
        qc_ref[h] = (jnp.concatenate([qn, qr], axis=-1) * (SM_SCALE * _LOG2E)).astype(qc_ref.dtype)
        kn = _rms(kv[:, h * 256:h * 256 + NOPE_DIM], knn_ref[...])
        kc_ref[h] = jnp.concatenate([kn, kr], axis=-1).astype(kc_ref.dtype)
        vc_ref[h] = kv[:, h * 256 + NOPE_DIM:(h + 1) * 256].astype(vc_ref.dtype)


def _inproj(x, seq, cos2, sin2, lw):
    T = x.shape[0]
    tm = min(256, seq)
    nseq = seq // tm
    row = lambda i: (i, 0)
    full2 = lambda shape: pl.BlockSpec(shape, lambda i: (0, 0))
    in_specs = [
        pl.BlockSpec((tm, D_MODEL), row),
        pl.BlockSpec((tm, ROPE_DIM), lambda i: (i % nseq, 0)),
        pl.BlockSpec((tm, ROPE_DIM), lambda i: (i % nseq, 0)),
        full2((1, D_MODEL)),
        _vmem_full(),
        full2((1, LANES)), full2((1, LANES)), full2((1, SG_WIDTH)), full2((1, Q_LORA)),
        _vmem_full(),
        full2((1, KV_LORA)),
        _vmem_full(),
        full2((1, NOPE_DIM)), full2((1, ROPE_DIM)), full2((1, NOPE_DIM)), full2((1, ROPE_DIM)),
    ]
    head3 = lambda w: pl.BlockSpec((H_C, tm, w), lambda i: (0, i, 0))
    out_shape = (
        jax.ShapeDtypeStruct((T, QKV_A_COLS), F32),
        jax.ShapeDtypeStruct((T, LANES), F32),
        jax.ShapeDtypeStruct((T, 512), F32),
        jax.ShapeDtypeStruct((T, SG_WIDTH), F32),
        jax.ShapeDtypeStruct((T, SG_WIDTH), MXU_DTYPE),
        jax.ShapeDtypeStruct((H_C, T, QK_HEAD_DIM), MXU_DTYPE),
        jax.ShapeDtypeStruct((H_C, T, QK_HEAD_DIM), MXU_DTYPE),
        jax.ShapeDtypeStruct((H_C, T, DV_C), MXU_DTYPE),
        jax.ShapeDtypeStruct((T, N_BRANCH * D_MODEL), F32),
    )
    out_specs = (
        pl.BlockSpec((tm, QKV_A_COLS), row), pl.BlockSpec((tm, LANES), row), pl.BlockSpec((tm, 512), row),
        pl.BlockSpec((tm, SG_WIDTH), row), pl.BlockSpec((tm, SG_WIDTH), row),
        head3(QK_HEAD_DIM), head3(QK_HEAD_DIM), head3(DV_C),
        pl.BlockSpec((tm, N_BRANCH * D_MODEL), row),
    )
    return pl.pallas_call(
        _inproj_kernel, name="inproj", grid=(T // tm,), in_specs=in_specs, out_specs=out_specs, out_shape=out_shape,
        compiler_params=_cparams(("parallel",)),
    )(x, cos2, sin2, lw["mix_norm"], lw["w_in"], lw["a_log"], lw["dt_bias"], lw["sg_norm"], lw["q_a_norm"],
      lw["w_uq"], lw["kv_a_norm"], lw["w_ukv"], lw["q_nope_norm"], lw["q_rope_norm"], lw["k_nope_norm"],
      lw["k_rope_norm"])


_HALO = 8


def _conv_kernel(prev_ref, x_ref, next_ref, w_ref, q_ref, k_ref, v_ref, ext_ref):
    i = pl.program_id(1)
    n = pl.num_programs(1)
    ts = x_ref.shape[1]
    ext_ref[0:_HALO, :] = jnp.where(i > 0, prev_ref[0], 0.0)
    ext_ref[_HALO:_HALO + ts, :] = x_ref[0]
    ext_ref[_HALO + ts:, :] = jnp.where(i < n - 1, next_ref[0], 0.0)
    acc = None
    for j in range(CONV_W):
        start = _HALO - CONV_W // 2 + j
        term = ext_ref[start:start + ts, :] * w_ref[j:j + 1, :]
        acc = term if acc is None else acc + term
    y = _silu(acc)
    for h in range(H_A):
        qh = y[:, h * DK_A:(h + 1) * DK_A]
        kh = y[:, (H_A + h) * DK_A:(H_A + h + 1) * DK_A]
        q_ref[0, :, h * DK_A:(h + 1) * DK_A] = qh * (
            lax.rsqrt(jnp.sum(qh * qh, axis=-1, keepdims=True) + NORM_EPS) * (DK_A ** -0.5))
        k_ref[0, :, h * DK_A:(h + 1) * DK_A] = kh * lax.rsqrt(jnp.sum(kh * kh, axis=-1, keepdims=True) + NORM_EPS)
    v_ref[0] = y[:, 2 * H_A * DK_A:]


def _conv_prep(qkv, conv_w):
    B, S, C = qkv.shape
    ts = min(512, S)
    nb = S // ts
    hb = ts // _HALO
    nh = S // _HALO
    out = jax.ShapeDtypeStruct((B, S, H_A * DK_A), F32)
    blk = pl.BlockSpec((1, ts, H_A * DK_A), lambda b, i: (b, i, 0))
    return pl.pallas_call(
        _conv_kernel, name="conv_prep", grid=(B, nb),
        in_specs=[
            pl.BlockSpec((1, _HALO, C), lambda b, i: (b, jnp.maximum(i * hb - 1, 0), 0)),
            pl.BlockSpec((1, ts, C), lambda b, i: (b, i, 0)),
            pl.BlockSpec((1, _HALO, C), lambda b, i: (b, jnp.minimum((i + 1) * hb, nh - 1), 0)),
            pl.BlockSpec((CONV_W, C), lambda b, i: (0, 0)),
        ],
        out_specs=(blk, blk, blk), out_shape=(out, out, out),
        scratch_shapes=[pltpu.VMEM((ts + 2 * _HALO, C), F32)],
        compiler_params=_cparams(("parallel", "parallel")),
    )(qkv, qkv, qkv, conv_w)


def _delta_units(q, k, v, gam_c, gam_r, b_c, state, backward):
    C = DELTA_CHUNK
    U = range(len(q))
    r = lax.broadcasted_iota(jnp.int32, (C, C), 0)
    c = lax.broadcasted_iota(jnp.int32, (C, C), 1)
    incl = [r <= c if backward[u] else r >= c for u in U]
    strict = [r < c if backward[u] else r > c for u in U]
    last = [0 if backward[u] else C - 1 for u in U]
    decay = [jnp.where(incl[u], jnp.exp(jnp.where(incl[u], gam_c[u] - gam_r[u], 0.0)), 0.0) for u in U]
    kb = [k[u] * b_c[u] for u in U]
    a = [jnp.where(strict[u], _mm_nt(kb[u], k[u]) * decay[u], 0.0) for u in U]
    qk = [_mm_nt(q[u], k[u]) * decay[u] for u in U]
    eg = [jnp.exp(gam_c[u]) for u in U]
    y = [jnp.concatenate([v[u] * b_c[u], kb[u] * eg[u]], axis=-1) for u in U]
    y = [y[u] - _mm(a[u], y[u]) for u in U]
    p = a
    for _ in range(5):
        p = [_mm(p[u], p[u]) for u in U]
        y = [y[u] + _mm(p[u], y[u]) for u in U]
    g_last = [gam_c[u][last[u]:last[u] + 1, :] for u in U]
    k_dec = [k[u] * jnp.exp(g_last[u] - gam_c[u]) for u in U]
    q_dec = [q[u] * eg[u] for u in U]
    v_new = [y[u][:, :DV_A] - _mm(y[u][:, DV_A:], state[u]) for u in U]
    o_state = [_mm(q_dec[u], state[u]) for u in U]
    o = [o_state[u] + _mm(qk[u], v_new[u]) for u in U]
    new_state = [state[u] * jnp.exp(g_last[u]) + _mm_tn(k_dec[u], v_new[u]) for u in U]
    return o, new_state


def _delta_kernel(qf_ref, kf_ref, vf_ref, gf_ref, rf_ref, qb_ref, kb_ref, vb_ref, gb_ref, rb_ref,
                  of_ref, ob_ref, state_ref):
    C = DELTA_CHUNK
    nc = qf_ref.shape[1] // C

    @pl.when(pl.program_id(1) == 0)
    def _():
        state_ref[...] = jnp.zeros_like(state_ref)

    def body(ci, carry):
        args = [[] for _ in range(8)]
        dests = []
        for d, (q_ref, k_ref, v_ref, g_ref, r_ref, o_ref) in enumerate(
                ((qf_ref, kf_ref, vf_ref, gf_ref, rf_ref, of_ref), (qb_ref, kb_ref, vb_ref, gb_ref, rb_ref, ob_ref))):
            cc = ci if d == 0 else nc - 1 - ci
            rows = pl.ds(pl.multiple_of(cc * C, C), C)
            g = g_ref[0, rows, :]
            grow = r_ref[0, cc]
            for h in range(H_A):
                lanes = slice(h * DK_A, (h + 1) * DK_A)
                j = d * H_A + h
                unit = (q_ref[0, rows, lanes], k_ref[0, rows, lanes], v_ref[0, rows, lanes], g[:, j:j + 1],
                        grow[j:j + 1, :], g[:, 2 * H_A + j:2 * H_A + j + 1], state_ref[j], d == 1)
                for lst, val in zip(args, unit):
                    lst.append(val)
                dests.append((o_ref, rows, lanes, j))
        outs, states = _delta_units(*args)
        for (o_ref, rows, lanes, j), o, s_new in zip(dests, outs, states):
            o_ref[0, rows, lanes] = o
            state_ref[j] = s_new
        return carry

    lax.fori_loop(0, nc, body, 0)


def _delta_rule(q, k, v, gam, gam_rows):
    B, S, W = q.shape
    ts = min(256, S)
    nb = S // ts
    ncb = ts // DELTA_CHUNK
    fwd = lambda b, i: (b, i, 0)
    bwd = lambda b, i: (b, nb - 1 - i, 0)
    fwd4 = lambda b, i: (b, i, 0, 0)
    bwd4 = lambda b, i: (b, nb - 1 - i, 0, 0)

    def specs(m3, m4):
        return [pl.BlockSpec((1, ts, W), m3), pl.BlockSpec((1, ts, W), m3), pl.BlockSpec((1, ts, W), m3),
                pl.BlockSpec((1, ts, LANES), m3), pl.BlockSpec((1, ncb, 2 * H_A, DELTA_CHUNK), m4)]

    out = jax.ShapeDtypeStruct((B, S, W), F32)
    return pl.pallas_call(
        _delta_kernel, name="delta_rule", grid=(B, nb), in_specs=specs(fwd, fwd4) + specs(bwd, bwd4),
        out_specs=(pl.BlockSpec((1, ts, W), fwd), pl.BlockSpec((1, ts, W), bwd)), out_shape=(out, out),
        scratch_shapes=[pltpu.VMEM((2 * H_A, DK_A, DV_A), F32)],
        compiler_params=_cparams(("parallel", "arbitrary")),
    )(q, k, v, gam, gam_rows, q, k, v, gam, gam_rows)


def _attn_kernel(q_ref, k_ref, v_ref, o_ref, m_ref, l_ref, acc_ref):
    kv = pl.program_id(3)

    @pl.when(kv == 0)
    def _():
        m_ref[...] = jnp.full_like(m_ref, -jnp.inf)
        l_ref[...] = jnp.zeros_like(l_ref)
        acc_ref[...] = jnp.zeros_like(acc_ref)

    q = q_ref[0, 0]
    nk = k_ref.shape[2] // _ATTN_KSUB
    nt = _ATTN_KSUB // LANES
    ksub = lambda j: slice(j * _ATTN_KSUB, (j + 1) * _ATTN_KSUB)
    m = m_ref[...]
    l = l_ref[...]
    acc = acc_ref[...]
    s = _mm_nt(q, k_ref[0, 0, ksub(0), :])
    for j in range(nk):
        s_next = _mm_nt(q, k_ref[0, 0, ksub(j + 1), :]) if j + 1 < nk else None
        tiles = [s[:, t * LANES:(t + 1) * LANES] for t in range(nt)]
        m_new = jnp.maximum(m, jnp.max(functools.reduce(jnp.maximum, tiles), axis=-1, keepdims=True))
        alpha = jnp.exp2(m - m_new)
        ps = [jnp.exp2(t - m_new) for t in tiles]
        l = alpha * l + functools.reduce(lambda a, b: a + b, ps)
        p = jnp.concatenate([t.astype(MXU_DTYPE) for t in ps], axis=-1)
        acc = alpha * acc + jnp.dot(p, v_ref[0, 0, ksub(j), :], preferred_element_type=F32)
        m = m_new
        s = s_next
    m_ref[...] = m
    l_ref[...] = l
    acc_ref[...] = acc

    @pl.when(kv == pl.num_programs(3) - 1)
    def _():
        o_ref[0] = (acc / jnp.sum(l, axis=-1, keepdims=True)).astype(o_ref.dtype)


_ATTN_KSUB = 512


def _attention(qc, kc, vc, B, S):
    tq = min(1024, S)
    tk = min(4096, S)
    qc = qc.reshape(H_C, B, S, QK_HEAD_DIM)
    kc = kc.reshape(H_C, B, S, QK_HEAD_DIM)
    vc = vc.reshape(H_C, B, S, DV_C)
    return pl.pallas_call(
        _attn_kernel, name="attention", grid=(B, H_C, S // tq, S // tk),
        in_specs=[pl.BlockSpec((1, 1, tq, QK_HEAD_DIM), lambda b, h, i, j: (h, b, i, 0)),
                  pl.BlockSpec((1, 1, tk, QK_HEAD_DIM), lambda b, h, i, j: (h, b, j, 0)),
                  pl.BlockSpec((1, 1, tk, DV_C), lambda b, h, i, j: (h, b, j, 0))],
        out_specs=pl.BlockSpec((1, tq, DV_C), lambda b, h, i, j: (b, i, h)),
        out_shape=jax.ShapeDtypeStruct((B, S, H_C * DV_C), MXU_DTYPE),
        scratch_shapes=[pltpu.VMEM((tq, LANES), F32), pltpu.VMEM((tq, LANES), F32), pltpu.VMEM((tq, DV_C), F32)],
        compiler_params=_cparams(("parallel", "parallel", "parallel", "arbitrary")),
    )(qc, kc, vc)


def _merge_kernel(x_ref, of_ref, ob_ref, sz_ref, ub_ref, vb_ref, oc_ref, gate_ref, onorm_ref, sgw_ref, sgb_ref,
                  wbr_ref, wout_ref, h_ref):
    tm = x_ref.shape[0]
    o = of_ref[...] + ob_ref[...]
    sz = sz_ref[...]
    out_a = jnp.concatenate(
        [_rms(o[:, h * DV_A:(h + 1) * DV_A], onorm_ref[...]) * sz[:, h * DV_A:(h + 1) * DV_A] for h in range(H_A)],
        axis=-1)
    gw = SG_WIDTH // SG_GROUPS
    chunks = []
    for ci in range(tm // SG_CHUNK):
        rows = slice(ci * SG_CHUNK, (ci + 1) * SG_CHUNK)
        mixed = jnp.concatenate(
            [jnp.dot(sgw_ref[g], vb_ref[rows, g * gw:(g + 1) * gw], preferred_element_type=F32) + sgb_ref[:, g:g + 1]
             for g in range(SG_GROUPS)], axis=-1)
        chunks.append(ub_ref[rows, :] * mixed)
    out_b = jnp.concatenate(chunks, axis=0)
    gate = gate_ref[...]
    merged = (gate[:, :D_MODEL] * _mm(out_a, wbr_ref[0])
              + gate[:, D_MODEL:2 * D_MODEL] * _mm(out_b, wbr_ref[1])
              + gate[:, 2 * D_MODEL:] * jnp.dot(oc_ref[...], wbr_ref[2], preferred_element_type=F32))
    h_ref[...] = x_ref[...] + _mm(merged, wout_ref[...])


def _merge(x, o_f, o_b, sz, ub, vb, oc, gate, lw):
    T = x.shape[0]
    tm = min(256, T)
    row = lambda i: (i, 0)
    w512 = pl.BlockSpec((tm, 512), row)
    return pl.pallas_call(
        _merge_kernel, name="merge", grid=(T // tm,),
        in_specs=[pl.BlockSpec((tm, D_MODEL), row), w512, w512, w512, w512, w512, w512,
                  pl.BlockSpec((tm, N_BRANCH * D_MODEL), row),
                  pl.BlockSpec((1, DV_A), lambda i: (0, 0)),
                  _vmem_full(), _vmem_full(), _vmem_full(), _vmem_full()],
        out_specs=pl.BlockSpec((tm, D_MODEL), row),
        out_shape=jax.ShapeDtypeStruct((T, D_MODEL), F32),
        compiler_params=_cparams(("parallel",)),
    )(x, o_f, o_b, sz, ub, vb, oc, gate, lw["o_norm"], lw["sg_w"], lw["sg_bT"], lw["w_branch"], lw["w_out"])


_NEG_INF = float("-inf")
_CAND_B_LIMIT = tuple(PEER_TOPK // (a + 1) for a in range(PEER_TOPK))


GATE_DTYPE = jnp.bfloat16


def _pack_gate_pair(lo, hi):
    lo_bits = lax.bitcast_convert_type(lo.astype(GATE_DTYPE).astype(F32), jnp.uint32)
    hi_bits = lax.bitcast_convert_type(hi.astype(GATE_DTYPE).astype(F32), jnp.uint32)
    return (lo_bits >> 16) | (hi_bits & jnp.uint32(0xFFFF0000))


def _top16_rows(s, top_ref, exact):
    n, tt = s.shape
    rid = lax.broadcasted_iota(jnp.int32, (n, tt), 0)
    work = s
    pos = jnp.full((n, tt), float(PEER_TOPK), F32)
    for r in range(PEER_TOPK):
        m = jnp.max(work, axis=0, keepdims=True)
        sel = work == m
        if exact:
            sel = rid == jnp.min(jnp.where(sel, rid, n), axis=0, keepdims=True)
        pos = jnp.where(sel, float(r), pos)
        work = jnp.where(sel, _NEG_INF, work)
        top_ref[r:r + 1, :] = m
    return pos


def _peer_select_head(h, s0, s1, exact, pos1_ref, bm_ref, nbrow_ref, arow_ref, t0_ref, t1_ref):
    tt = s0.shape[1]
    sub = lax.broadcasted_iota(jnp.int32, (8, tt), 0)
    pos0 = _top16_rows(s0, t0_ref, exact)
    pos1 = _top16_rows(s1, t1_ref, exact)
    t0 = t0_ref[...]
    t1 = t1_ref[...]
    t00 = t0[0:1, :]
    t10 = t1[0:1, :]
    c00 = t00 + t10
    vals = [t00 + t1[0:8, :], t00 + t1[8:16, :]]
    cidx = [sub, sub + 8]
    for a in range(1, 8):
        vals.append(jnp.where(sub < _CAND_B_LIMIT[a], t0[a:a + 1, :] + t1[0:8, :], _NEG_INF))
        cidx.append(sub + a * PEER_TOPK)
    vals.append(t0[8:16, :] + t10)
    cidx.append((sub + 8) * PEER_TOPK)
    orig = list(vals)
    big = PEER_TOPK * PEER_TOPK
    for _ in range(PEER_TOPK):
        m = functools.reduce(jnp.maximum, vals)
        m = jnp.max(m, axis=0, keepdims=True)
        if exact:
            cand = functools.reduce(jnp.minimum, [jnp.where(v == m, ci, big) for v, ci in zip(vals, cidx)])
            idx = jnp.min(cand, axis=0, keepdims=True)
            vals = [jnp.where(ci == idx, _NEG_INF, v) for v, ci in zip(vals, cidx)]
        else:
            vals = [jnp.where(v == m, _NEG_INF, v) for v in vals]
    sel = [jnp.where(v != o, 1.0, 0.0) for v, o in zip(vals, orig)]
    z = functools.reduce(
        lambda x, y: x + y,
        [jnp.sum(jnp.where(s > 0.0, jnp.exp(jnp.where(s > 0.0, o - c00, 0.0)), 0.0), axis=0, keepdims=True)
         for s, o in zip(sel, orig)])
    nb_lo = [jnp.sum(sel[0], axis=0, keepdims=True) + jnp.sum(sel[1], axis=0, keepdims=True)]
    for a in range(1, 8):
        nb_lo.append(jnp.sum(sel[a + 1], axis=0, keepdims=True))
    nb_hi = sel[9]
    nbrow = jnp.zeros((N_KEYS, tt), F32)
    for a in range(PEER_TOPK):
        nba = nb_lo[a] if a < 8 else nb_hi[a - 8:a - 7, :]
        nbrow = jnp.where(pos0 == float(a), nba, nbrow)
    inv_z = 1.0 / z
    in0 = pos0 < float(PEER_TOPK)
    in1 = pos1 < float(PEER_TOPK)
    arow = jnp.where(in0, jnp.exp(jnp.where(in0, s0 - t00, 0.0)), 0.0) * inv_z
    bm = jnp.where(in1, jnp.exp(jnp.where(in1, s1 - t10, 0.0)), 0.0)
    nbrow_ref[h] = _pack_gate_pair(nbrow, nbrow)
    arow_ref[h] = _pack_gate_pair(arow, arow)
    half = N_KEYS // 2
    pos1_ref[h] = _pack_gate_pair(pos1[:half], pos1[half:])
    bm_ref[h] = _pack_gate_pair(bm[:half], bm[half:])
    k = float(PEER_TOPK)
    count = lambda flags: jnp.sum(jnp.where(flags, 1.0, 0.0), axis=0, keepdims=True)
    chosen = functools.reduce(lambda x, y: x + y, nb_lo) + jnp.sum(nb_hi, axis=0, keepdims=True)
    return jnp.abs(count(in0) - k) + jnp.abs(count(in1) - k) + jnp.abs(chosen - k)


def _peer_topk_kernel(h_ref, gain_ref, wq_ref, keys_ref, xn_ref, pos1_ref, bm_ref, nbrow_ref, arow_ref,
                      q_ref, t0_ref, t1_ref):
    xn = _rms(h_ref[...], gain_ref[...]).astype(MXU_DTYPE)
    xn_ref[...] = xn
    q_ref[...] = jnp.dot(xn, wq_ref[...], preferred_element_type=F32).astype(MXU_DTYPE)
    outs = (pos1_ref, bm_ref, nbrow_ref, arow_ref, t0_ref, t1_ref)

    def head(h, carry):
        base = pl.multiple_of(h * PEER_QDIM, PEER_QDIM)
        s0 = _mm_nt(keys_ref[h, 0], q_ref[:, pl.ds(base, N_KEYS)])
        s1 = _mm_nt(keys_ref[h, 1], q_ref[:, pl.ds(base + N_KEYS, N_KEYS)])
        off = _peer_select_head(h, s0, s1, False, *outs)

        @pl.when(jnp.max(off) > 0.0)
        def _():
            _peer_select_head(h, s0, s1, True, *outs)

        return carry

    lax.fori_loop(0, PEER_HEADS, head, 0)


def _peer_topk(h, lw):
    T = h.shape[0]
    tt = min(256, T)
    half = N_KEYS // 2
    rows_meta = jax.ShapeDtypeStruct((PEER_HEADS, N_KEYS, T), jnp.uint32)
    pair_meta = jax.ShapeDtypeStruct((PEER_HEADS, half, T), jnp.uint32)
    rows_spec = pl.BlockSpec((PEER_HEADS, N_KEYS, tt), lambda i: (0, 0, i))
    pair_spec = pl.BlockSpec((PEER_HEADS, half, tt), lambda i: (0, 0, i))
    return pl.pallas_call(
        _peer_topk_kernel, name="peer_topk", grid=(T // tt,),
        in_specs=[pl.BlockSpec((tt, D_MODEL), lambda i: (i, 0)), pl.BlockSpec((1, D_MODEL), lambda i: (0, 0)),
                  _vmem_full(), _vmem_full()],
        out_specs=(pl.BlockSpec((tt, D_MODEL), lambda i: (i, 0)), pair_spec, pair_spec, rows_spec, rows_spec),
        out_shape=(jax.ShapeDtypeStruct((T, D_MODEL), MXU_DTYPE), pair_meta, pair_meta, rows_meta, rows_meta),
        scratch_shapes=[pltpu.VMEM((tt, PEER_HEADS * PEER_QDIM), MXU_DTYPE), pltpu.VMEM((PEER_TOPK, tt), F32),
                        pltpu.VMEM((PEER_TOPK, tt), F32)],
        compiler_params=_cparams(("parallel",)),
    )(h, lw["ffn_norm"], lw["peer_wq"], lw["peer_keys"])


_PEER_IB = 8
_PEER_CHUNK_KEYS = 8


def _peer_dense_kernel(h_ref, xn_ref, u_ref, vt_ref, pos1_ref, bm_ref, nbrow_ref, arow_ref, o_ref,
                       acc_ref, hid0_ref, hid1_ref, gh_ref):
    j = pl.program_id(1)
    nblk = pl.num_programs(1) - 1
    tt = xn_ref.shape[0]
    half = N_KEYS // 2
    unpack = lambda words: pltpu.bitcast(words, GATE_DTYPE)

    nchunk = _PEER_IB // _PEER_CHUNK_KEYS
    crows = _PEER_CHUNK_KEYS * N_KEYS

    def hidden(dst_ref, c):
        rows = slice(c * crows, (c + 1) * crows)
        dst_ref[rows, :] = _mm_nt(u_ref[rows, :], xn_ref[...])

    def gate_and_project(src_ref, c):
        row0 = pl.ds(pl.multiple_of((j - 1) * _PEER_IB, _PEER_IB), _PEER_IB)
        for il in range(c * _PEER_CHUNK_KEYS, (c + 1) * _PEER_CHUNK_KEYS):
            rows = slice(il * N_KEYS, (il + 1) * N_KEYS)
            for st in range(tt // LANES):
                lanes = slice(st * LANES, (st + 1) * LANES)
                g = None
                for h in range(PEER_HEADS):
                    nb = unpack(jnp.broadcast_to(nbrow_ref[h, row0, lanes][il:il + 1, :], (half, LANES)))
                    ar = unpack(jnp.broadcast_to(arow_ref[h, row0, lanes][il:il + 1, :], (half, LANES)))
                    term = jnp.where(unpack(pos1_ref[h, :, lanes]) < nb, unpack(bm_ref[h, :, lanes]), 0) * ar
                    g = term if g is None else g + term
                gh_ref[rows, lanes] = g * _gelu(src_ref[rows, lanes]).astype(GATE_DTYPE)
        rows = slice(c * crows, (c + 1) * crows)
        acc_ref[...] += jnp.dot(vt_ref[:, rows], gh_ref[rows, :].astype(vt_ref.dtype), preferred_element_type=F32)

    def step(dst_ref, src_ref):
        for c in range(nchunk):
            if dst_ref is not None:
                hidden(dst_ref, c)
            if src_ref is not None:
                gate_and_project(src_ref, c)

    @pl.when(j == 0)
    def _():
        acc_ref[...] = jnp.zeros_like(acc_ref)
        step(hid0_ref, None)

    @pl.when((j > 0) & (j < nblk) & (j % 2 == 1))
    def _():
        step(hid1_ref, hid0_ref)

    @pl.when((j > 0) & (j < nblk) & (j % 2 == 0))
    def _():
        step(hid0_ref, hid1_ref)

    @pl.when(j == nblk)
    def _():
        step(None, hid1_ref)
        o_ref[...] = h_ref[...] + acc_ref[...].T


def _peer_dense(h, xn, pos1, bm, nbrow, arow, lw):
    T = h.shape[0]
    tt = min(512, T)
    eb = _PEER_IB * N_KEYS
    nblk = N_KEYS // _PEER_IB
    assert nblk % 2 == 0
    rows_spec = pl.BlockSpec((PEER_HEADS, N_KEYS, tt), lambda i, j: (0, 0, i))
    pair_spec = pl.BlockSpec((PEER_HEADS, N_KEYS // 2, tt), lambda i, j: (0, 0, i))
    return pl.pallas_call(
        _peer_dense_kernel, name="peer_dense", grid=(T // tt, nblk + 1),
        in_specs=[pl.BlockSpec((tt, D_MODEL), lambda i, j: (i, 0)), pl.BlockSpec((tt, D_MODEL), lambda i, j: (i, 0)),
                  pl.BlockSpec((eb, D_MODEL), lambda i, j: (jnp.minimum(j, nblk - 1), 0)),
                  pl.BlockSpec((D_MODEL, eb), lambda i, j: (0, jnp.maximum(j - 1, 0))),
                  pair_spec, pair_spec, rows_spec, rows_spec],
        out_specs=pl.BlockSpec((tt, D_MODEL), lambda i, j: (i, 0)),
        out_shape=jax.ShapeDtypeStruct((T, D_MODEL), F32),
        scratch_shapes=[pltpu.VMEM((D_MODEL, tt), F32), pltpu.VMEM((eb, tt), F32), pltpu.VMEM((eb, tt), F32),
                        pltpu.VMEM((eb, tt), GATE_DTYPE)],
        compiler_params=_cparams(("parallel", "arbitrary")),
    )(h, xn, lw["peer_u"], lw["peer_vT"], pos1, bm, nbrow, arow)


def _pad_cols(w, width):
    return jnp.pad(w, ((0, 0), (0, width - w.shape[1])))


def _layer_weights(l, mix_norm, w_in, conv_w, a_log, dt_bias, o_norm, sg_norm, sg_w, sg_b, q_a_norm, w_uq, kv_a_norm,
                   w_ukv, q_nope_norm, q_rope_norm, k_nope_norm, k_rope_norm, w_branch, w_out, ffn_norm, peer_wq,
                   peer_keys, peer_u, peer_v):
    split = [QKV_A_COLS, 2 * H_A, 2 * H_A, H_A * DV_A, SG_WIDTH, SG_WIDTH, Q_LORA, KV_LORA, ROPE_DIM,
             N_BRANCH * D_MODEL]
    offs = [0]
    for s in split:
        offs.append(offs[-1] + s)
    wl = w_in[l]
    piece = lambda i: wl[:, offs[i]:offs[i + 1]]
    w_cat = jnp.concatenate([
        piece(0), _pad_cols(jnp.concatenate([piece(1), piece(2)], axis=1), LANES), piece(3), piece(4), piece(5),
        piece(6), piece(7), _pad_cols(piece(8), LANES), piece(9)], axis=1).astype(MXU_DTYPE)
    wq3 = w_uq[l].reshape(Q_LORA, H_C, QK_HEAD_DIM)
    w_uq_r = jnp.concatenate([wq3[:, :, :NOPE_DIM].reshape(Q_LORA, H_C * NOPE_DIM),
                              wq3[:, :, NOPE_DIM:].reshape(Q_LORA, H_C * ROPE_DIM)], axis=1).astype(MXU_DTYPE)
    row = lambda v: v.reshape(1, -1).astype(F32)
    return {
        "mix_norm": row(mix_norm[l]), "w_in": w_cat, "conv_w": conv_w[l].astype(F32),
        "a_log": _pad_cols(row(a_log[l]), LANES), "dt_bias": _pad_cols(row(dt_bias[l]), LANES),
        "o_norm": row(o_norm[l]), "sg_norm": row(sg_norm[l]), "sg_w": sg_w[l].astype(MXU_DTYPE),
        "sg_bT": sg_b[l].T.astype(F32), "q_a_norm": row(q_a_norm[l]), "w_uq": w_uq_r,
        "kv_a_norm": row(kv_a_norm[l]), "w_ukv": w_ukv[l].astype(MXU_DTYPE), "q_nope_norm": row(q_nope_norm[l]),
        "q_rope_norm": row(q_rope_norm[l]), "k_nope_norm": row(k_nope_norm[l]), "k_rope_norm": row(k_rope_norm[l]),
        "w_branch": w_branch[l].astype(MXU_DTYPE), "w_out": w_out[l].astype(MXU_DTYPE), "ffn_norm": row(ffn_norm[l]),
        "peer_wq": peer_wq[l].astype(MXU_DTYPE), "peer_keys": peer_keys[l].astype(MXU_DTYPE),
        "peer_u": _pair_order(peer_u[l].astype(MXU_DTYPE)), "peer_vT": _pair_order(peer_v[l].astype(MXU_DTYPE)).T,
    }


def _pair_order(table):
    half = N_KEYS // 2
    return table.reshape(N_KEYS, 2, half, D_MODEL).swapaxes(1, 2).reshape(N_KEYS * N_KEYS, D_MODEL)


def _rope_tables(S):
    inv_freq = ROPE_THETA ** (-jnp.arange(0, ROPE_DIM, 2, dtype=F32) / ROPE_DIM)
    ang = jnp.arange(S, dtype=F32)[:, None] * inv_freq[None, :]
    cos, sin = jnp.cos(ang), jnp.sin(ang)
    return jnp.concatenate([cos, cos], axis=-1), jnp.concatenate([-sin, sin], axis=-1)


def _mixer(x, B, S, tables, lw):
    qkv, gam, sz, ub, vb, qc, kc, vc, gate = _inproj(x, S, tables[0], tables[1], lw)
    q, k, v = _conv_prep(qkv.reshape(B, S, QKV_A_COLS), lw["conv_w"])
    nch = S // DELTA_CHUNK
    gam_rows = jnp.swapaxes(gam[:, :2 * H_A].reshape(B, nch, DELTA_CHUNK, 2 * H_A), 2, 3)
    o_f, o_b = _delta_rule(q, k, v, gam.reshape(B, S, LANES), gam_rows)
    oc = _attention(qc, kc, vc, B, S)
    T = B * S
    return _merge(x, o_f.reshape(T, -1), o_b.reshape(T, -1), sz, ub, vb, oc.reshape(T, -1), gate, lw)


def _peer(h, lw):
    xn, pos1, bm, nbrow, arow = _peer_topk(h, lw)
    return _peer_dense(h, xn, pos1, bm, nbrow, arow, lw)


def kernel(x_prompt, x_sample, mix_norm, w_in, conv_w, a_log, dt_bias, o_norm, sg_norm, sg_w, sg_b, q_a_norm, w_uq,
           kv_a_norm, w_ukv, q_nope_norm, q_rope_norm, k_nope_norm, k_rope_norm, w_branch, w_out, ffn_norm, peer_wq,
           peer_keys, peer_u, peer_v):
    params = (mix_norm, w_in, conv_w, a_log, dt_bias, o_norm, sg_norm, sg_w, sg_b, q_a_norm, w_uq, kv_a_norm, w_ukv,
              q_nope_norm, q_rope_norm, k_nope_norm, k_rope_norm, w_branch, w_out, ffn_norm, peer_wq, peer_keys,
              peer_u, peer_v)
    depth = w_in.shape[0]
    trunks = [x_prompt, x_sample]
    shapes = [t.shape for t in trunks]
    acts = [t.reshape(-1, D_MODEL) for t in trunks]
    tables = [_rope_tables(s[1]) for s in shapes]
    sizes = [a.shape[0] for a in acts]
    for l in range(depth):
        lw = _layer_weights(l, *params)
        mixed = [_mixer(a, s[0], s[1], t, lw) for a, s, t in zip(acts, shapes, tables)]
        y = _peer(jnp.concatenate(mixed, axis=0), lw)
        acts = [y[:sizes[0]], y[sizes[0]:]]
    return tuple(a.reshape(s) for a, s in zip(acts, shapes))
```

```python
import functools
import math

import jax
import jax.numpy as jnp
from jax import lax
from jax.experimental import pallas as pl
from jax.experimental.pallas import tpu as pltpu

D_MODEL = 1024
NORM_EPS = 1e-6
H_A = 4
DK_A = 128
DV_A = 128
CONV_W = 5
DELTA_CHUNK = 64
QKV_A_COLS = 3 * H_A * DK_A
SG_GROUPS = 4
SG_CHUNK = 128
SG_WIDTH = 512
H_C = 4
Q_LORA = 384
KV_LORA = 256
NOPE_DIM = 128
ROPE_DIM = 64
DV_C = 128
QK_HEAD_DIM = NOPE_DIM + ROPE_DIM
ROPE_THETA = 10000.0
SM_SCALE = QK_HEAD_DIM ** -0.5
_LOG2E = math.log2(math.e)
PEER_HEADS = 8
N_KEYS = 128
PEER_QDIM = 256
PEER_TOPK = 16
N_BRANCH = 3

MXU_DTYPE = jnp.bfloat16
LANES = 128
VMEM_LIMIT = 56 * 1024 * 1024

_IN_GROUPS = (("qkv", QKV_A_COLS), ("ab", LANES), ("z", 512), ("u", SG_WIDTH), ("v", SG_WIDTH),
              ("cq", Q_LORA), ("ckv", KV_LORA), ("kr", LANES), ("gate", N_BRANCH * D_MODEL))
_IN_OFF = {}
_off = 0
for _name, _w in _IN_GROUPS:
    _IN_OFF[_name] = (_off, _off + _w)
    _off += _w
IN_COLS_PAD = _off

F32 = jnp.float32


def _mm(a, b):
    return jnp.dot(a.astype(MXU_DTYPE), b.astype(MXU_DTYPE), preferred_element_type=F32)


def _mm_nt(a, b):
    return lax.dot_general(a.astype(MXU_DTYPE), b.astype(MXU_DTYPE), (((1,), (1,)), ((), ())),
                           preferred_element_type=F32)


def _mm_tn(a, b):
    return lax.dot_general(a.astype(MXU_DTYPE), b.astype(MXU_DTYPE), (((0,), (0,)), ((), ())),
                           preferred_element_type=F32)


def _rms(x, gain):
    return x * lax.rsqrt(jnp.mean(x * x, axis=-1, keepdims=True) + NORM_EPS) * gain


def _sigmoid(x):
    return 1.0 / (1.0 + jnp.exp(-x))


def _silu(x):
    return x * _sigmoid(x)


def _gelu(x):
    return x * (0.5 * (1.0 + jnp.tanh(math.sqrt(2.0 / math.pi) * (x + 0.044715 * (x * x * x)))))


def _softplus(x):
    return jnp.maximum(x, 0.0) + jnp.log1p(jnp.exp(-jnp.abs(x)))


def _cparams(sem):
    return pltpu.CompilerParams(dimension_semantics=sem, vmem_limit_bytes=VMEM_LIMIT)


def _vmem_full():
    return pl.BlockSpec(memory_space=pltpu.VMEM)


def _inproj_kernel(x_ref, cos_ref, sin_ref, mixg_ref, win_ref, alog_ref, dtb_ref, sgn_ref, qan_ref, wuq_ref,
                   kvan_ref, wukv_ref, qnn_ref, qrn_ref, knn_ref, krn_ref,
                   qkv_ref, gam_ref, sz_ref, ub_ref, vb_ref, qc_ref, kc_ref, vc_ref, gate_ref):
    tm = x_ref.shape[0]
    xn = _rms(x_ref[...], mixg_ref[...]).astype(MXU_DTYPE)

    def proj(name):
        a, b = _IN_OFF[name]
        return jnp.dot(xn, win_ref[:, a:b], preferred_element_type=F32)

    qkv_ref[...] = proj("qkv")
    sz_ref[...] = _silu(proj("z"))
    ub_ref[...] = _gelu(proj("u"))
    vb_ref[...] = _rms(_gelu(proj("v")), sgn_ref[...]).astype(vb_ref.dtype)
    gate_ref[...] = _sigmoid(proj("gate"))

    ab = proj("ab")
    col = lax.broadcasted_iota(jnp.int32, ab.shape, 1)
    la = jnp.where(col < 2 * H_A, -jnp.exp(alog_ref[...]) * _softplus(ab + dtb_ref[...]), 0.0)
    beta = _sigmoid(ab)
    r = lax.broadcasted_iota(jnp.int32, (tm, tm), 0)
    c = lax.broadcasted_iota(jnp.int32, (tm, tm), 1)
    same = (r // DELTA_CHUNK) == (c // DELTA_CHUNK)
    tri_f = jnp.where(same & (r >= c), 1.0, 0.0).astype(MXU_DTYPE)
    tri_b = jnp.where(same & (r <= c), 1.0, 0.0).astype(MXU_DTYPE)
    p0 = la.astype(MXU_DTYPE)
    r1 = la - p0.astype(F32)
    p1 = r1.astype(MXU_DTYPE)
    p2 = (r1 - p1.astype(F32)).astype(MXU_DTYPE)
    gf = (jnp.dot(tri_f, p0, preferred_element_type=F32) + jnp.dot(tri_f, p1, preferred_element_type=F32)
          + jnp.dot(tri_f, p2, preferred_element_type=F32))
    gb = (jnp.dot(tri_b, p0, preferred_element_type=F32) + jnp.dot(tri_b, p1, preferred_element_type=F32)
          + jnp.dot(tri_b, p2, preferred_element_type=F32))
    gam = jnp.where(col < H_A, gf, gb)
    gam_ref[...] = jnp.where(col < 2 * H_A, gam, jnp.where(col < 4 * H_A, beta, 0.0))

    cos2 = cos_ref[...]
    sin2 = sin_ref[...]

    def rope(t):
        swapped = jnp.concatenate([t[:, ROPE_DIM // 2:], t[:, :ROPE_DIM // 2]], axis=-1)
        return t * cos2 + swapped * sin2

    q = _mm(_rms(proj("cq"), qan_ref[...]), wuq_ref[...])
    kv = _mm(_rms(proj("ckv"), kvan_ref[...]), wukv_ref[...])
    kr = rope(_rms(proj("kr")[:, :ROPE_DIM], krn_ref[...]))
    for h in range(H_C):
        qn = _rms(q[:, h * NOPE_DIM:(h + 1) * NOPE_DIM], qnn_ref[...])
        o = H_C * NOPE_DIM + h * ROPE_DIM
        qr = rope(_rms(q[:, o:o + ROPE_DIM], qrn_ref[...]))
        qc_ref[h] = (jnp.concatenate([qn, qr], axis=-1) * (SM_SCALE * _LOG2E)).astype(qc_ref.dtype)
        kn = _rms(kv[:, h * 256:h * 256 + NOPE_DIM], knn_ref[...])
        kc_ref[h] = jnp.concatenate([kn, kr], axis=-1).astype(kc_ref.dtype)
        vc_ref[h] = kv[:, h * 256 + NOPE_DIM:(h + 1) * 256].astype(vc_ref.dtype)


def _inproj(x, seq, cos2, sin2, lw):
    T = x.shape[0]
    tm = min(256, seq)
    nseq = seq // tm
    row = lambda i: (i, 0)
    full2 = lambda shape: pl.BlockSpec(shape, lambda i: (0, 0))
    in_specs = [
        pl.BlockSpec((tm, D_MODEL), row),
        pl.BlockSpec((tm, ROPE_DIM), lambda i: (i % nseq, 0)),
        pl.BlockSpec((tm, ROPE_DIM), lambda i: (i % nseq, 0)),
        full2((1, D_MODEL)),
        _vmem_full(),
        full2((1, LANES)), full2((1, LANES)), full2((1, SG_WIDTH)), full2((1, Q_LORA)),
        _vmem_full(),
        full2((1, KV_LORA)),
        _vmem_full(),
        full2((1, NOPE_DIM)), full2((1, ROPE_DIM)), full2((1, NOPE_DIM)), full2((1, ROPE_DIM)),
    ]
    head3 = lambda w: pl.BlockSpec((H_C, tm, w), lambda i: (0, i, 0))
    out_shape = (
        jax.ShapeDtypeStruct((T, QKV_A_COLS), F32),
        jax.ShapeDtypeStruct((T, LANES), F32),
        jax.ShapeDtypeStruct((T, 512), F32),
        jax.ShapeDtypeStruct((T, SG_WIDTH), F32),
        jax.ShapeDtypeStruct((T, SG_WIDTH), MXU_DTYPE),
        jax.ShapeDtypeStruct((H_C, T, QK_HEAD_DIM), MXU_DTYPE),
        jax.ShapeDtypeStruct((H_C, T, QK_HEAD_DIM), MXU_DTYPE),
        jax.ShapeDtypeStruct((H_C, T, DV_C), MXU_DTYPE),
        jax.ShapeDtypeStruct((T, N_BRANCH * D_MODEL), F32),
    )
    out_specs = (
        pl.BlockSpec((tm, QKV_A_COLS), row), pl.BlockSpec((tm, LANES), row), pl.BlockSpec((tm, 512), row),
        pl.BlockSpec((tm, SG_WIDTH), row), pl.BlockSpec((tm, SG_WIDTH), row),
        head3(QK_HEAD_DIM), head3(QK_HEAD_DIM), head3(DV_C),
        pl.BlockSpec((tm, N_BRANCH * D_MODEL), row),
    )
    return pl.pallas_call(
        _inproj_kernel, name="inproj", grid=(T // tm,), in_specs=in_specs, out_specs=out_specs, out_shape=out_shape,
        compiler_params=_cparams(("parallel",)),
    )(x, cos2, sin2, lw["mix_norm"], lw["w_in"], lw["a_log"], lw["dt_bias"], lw["sg_norm"], lw["q_a_norm"],
      lw["w_uq"], lw["kv_a_norm"], lw["w_ukv"], lw["q_nope_norm"], lw["q_rope_norm"], lw["k_nope_norm"],
      lw["k_rope_norm"])


_HALO = 8


def _conv_kernel(prev_ref, x_ref, next_ref, w_ref, q_ref, k_ref, v_ref, ext_ref):
    i = pl.program_id(1)
    n = pl.num_programs(1)
    ts = x_ref.shape[1]
    ext_ref[0:_HALO, :] = jnp.where(i > 0, prev_ref[0], 0.0)
    ext_ref[_HALO:_HALO + ts, :] = x_ref[0]
    ext_ref[_HALO + ts:, :] = jnp.where(i < n - 1, next_ref[0], 0.0)
    acc = None
    for j in range(CONV_W):
        start = _HALO - CONV_W // 2 + j
        term = ext_ref[start:start + ts, :] * w_ref[j:j + 1, :]
        acc = term if acc is None else acc + term
    y = _silu(acc)
    for h in range(H_A):
        qh = y[:, h * DK_A:(h + 1) * DK_A]
        kh = y[:, (H_A + h) * DK_A:(H_A + h + 1) * DK_A]
        q_ref[0, :, h * DK_A:(h + 1) * DK_A] = qh * (
            lax.rsqrt(jnp.sum(qh * qh, axis=-1, keepdims=True) + NORM_EPS) * (DK_A ** -0.5))
        k_ref[0, :, h * DK_A:(h + 1) * DK_A] = kh * lax.rsqrt(jnp.sum(kh * kh, axis=-1, keepdims=True) + NORM_EPS)
    v_ref[0] = y[:, 2 * H_A * DK_A:]


def _conv_prep(qkv, conv_w):
    B, S, C = qkv.shape
    ts = min(512, S)
    nb = S // ts
    hb = ts // _HALO
    nh = S // _HALO
    out = jax.ShapeDtypeStruct((B, S, H_A * DK_A), F32)
    blk = pl.BlockSpec((1, ts, H_A * DK_A), lambda b, i: (b, i, 0))
    return pl.pallas_call(
        _conv_kernel, name="conv_prep", grid=(B, nb),
        in_specs=[
            pl.BlockSpec((1, _HALO, C), lambda b, i: (b, jnp.maximum(i * hb - 1, 0), 0)),
            pl.BlockSpec((1, ts, C), lambda b, i: (b, i, 0)),
            pl.BlockSpec((1, _HALO, C), lambda b, i: (b, jnp.minimum((i + 1) * hb, nh - 1), 0)),
            pl.BlockSpec((CONV_W, C), lambda b, i: (0, 0)),
        ],
        out_specs=(blk, blk, blk), out_shape=(out, out, out),
        scratch_shapes=[pltpu.VMEM((ts + 2 * _HALO, C), F32)],
        compiler_params=_cparams(("parallel", "parallel")),
    )(qkv, qkv, qkv, conv_w)


_DELTA_CHUNKS_PER_ITER = 4


def _delta_units(q, k, v, gam_c, gam_r, b_c, state, backward):
    C = DELTA_CHUNK
    U = range(len(q))
    r = lax.broadcasted_iota(jnp.int32, (C, C), 0)
    c = lax.broadcasted_iota(jnp.int32, (C, C), 1)
    incl = [r <= c if backward[u] else r >= c for u in U]
    strict = [r < c if backward[u] else r > c for u in U]
    last = [0 if backward[u] else C - 1 for u in U]
    decay = [jnp.where(incl[u], jnp.exp(jnp.where(incl[u], gam_c[u] - gam_r[u], 0.0)), 0.0) for u in U]
    kb = [k[u] * b_c[u] for u in U]
    a = [jnp.where(strict[u], _mm_nt(kb[u], k[u]) * decay[u], 0.0) for u in U]
    qk = [_mm_nt(q[u], k[u]) * decay[u] for u in U]
    eg = [jnp.exp(gam_c[u]) for u in U]
    y = [jnp.concatenate([v[u] * b_c[u], kb[u] * eg[u]], axis=-1) for u in U]
    y = [y[u] - _mm(a[u], y[u]) for u in U]
    p = a
    for _ in range(5):
        p = [_mm(p[u], p[u]) for u in U]
        y = [y[u] + _mm(p[u], y[u]) for u in U]
    g_last = [gam_c[u][last[u]:last[u] + 1, :] for u in U]
    k_dec = [k[u] * jnp.exp(g_last[u] - gam_c[u]) for u in U]
    q_dec = [q[u] * eg[u] for u in U]
    outs = []
    for g0 in range(0, len(q), len(state)):
        G = range(len(state))
        v_new = [y[g0 + i][:, :DV_A] - _mm(y[g0 + i][:, DV_A:], state[i]) for i in G]
        o_state = [_mm(q_dec[g0 + i], state[i]) for i in G]
        outs += [o_state[i] + _mm(qk[g0 + i], v_new[i]) for i in G]
        state = [state[i] * jnp.exp(g_last[g0 + i]) + _mm_tn(k_dec[g0 + i], v_new[i]) for i in G]
    return outs, state


def _delta_kernel(qf_ref, kf_ref, vf_ref, gf_ref, rf_ref, qb_ref, kb_ref, vb_ref, gb_ref, rb_ref,
                  of_ref, ob_ref, state_ref):
    C = DELTA_CHUNK
    nc = qf_ref.shape[1] // C

    @pl.when(pl.program_id(1) == 0)
    def _():
        state_ref[...] = jnp.zeros_like(state_ref)

    per_iter = _DELTA_CHUNKS_PER_ITER if nc % _DELTA_CHUNKS_PER_ITER == 0 else 1

    def body(it, carry):
        args = [[] for _ in range(7)]
        dests = []
        for sub in range(per_iter):
            ci = it * per_iter + sub
            for d, (q_ref, k_ref, v_ref, g_ref, r_ref, o_ref) in enumerate(
                    ((qf_ref, kf_ref, vf_ref, gf_ref, rf_ref, of_ref),
                     (qb_ref, kb_ref, vb_ref, gb_ref, rb_ref, ob_ref))):
                cc = ci if d == 0 else nc - 1 - ci
                rows = pl.ds(pl.multiple_of(cc * C, C), C)
                g = g_ref[0, rows, :]
                grow = r_ref[0, cc]
                for h in range(H_A):
                    lanes = slice(h * DK_A, (h + 1) * DK_A)
                    j = d * H_A + h
                    unit = (q_ref[0, rows, lanes], k_ref[0, rows, lanes], v_ref[0, rows, lanes], g[:, j:j + 1],
                            grow[j:j + 1, :], g[:, 2 * H_A + j:2 * H_A + j + 1], d == 1)
                    for lst, val in zip(args, unit):
                        lst.append(val)
                    dests.append((o_ref, rows, lanes))
        states = [state_ref[j] for j in range(2 * H_A)]
        outs, states = _delta_units(*args[:6], states, args[6])
        for (o_ref, rows, lanes), o in zip(dests, outs):
            o_ref[0, rows, lanes] = o
        for j, s_new in enumerate(states):
            state_ref[j] = s_new
        return carry

    lax.fori_loop(0, nc // per_iter, body, 0)


def _delta_rule(q, k, v, gam, gam_rows):
    B, S, W = q.shape
    ts = min(256, S)
    nb = S // ts
    ncb = ts // DELTA_CHUNK
    fwd = lambda b, i: (b, i, 0)
    bwd = lambda b, i: (b, nb - 1 - i, 0)
    fwd4 = lambda b, i: (b, i, 0, 0)
    bwd4 = lambda b, i: (b, nb - 1 - i, 0, 0)

    def specs(m3, m4):
        return [pl.BlockSpec((1, ts, W), m3), pl.BlockSpec((1, ts, W), m3), pl.BlockSpec((1, ts, W), m3),
                pl.BlockSpec((1, ts, LANES), m3), pl.BlockSpec((1, ncb, 2 * H_A, DELTA_CHUNK), m4)]

    out = jax.ShapeDtypeStruct((B, S, W), F32)
    return pl.pallas_call(
        _delta_kernel, name="delta_rule", grid=(B, nb), in_specs=specs(fwd, fwd4) + specs(bwd, bwd4),
        out_specs=(pl.BlockSpec((1, ts, W), fwd), pl.BlockSpec((1, ts, W), bwd)), out_shape=(out, out),
        scratch_shapes=[pltpu.VMEM((2 * H_A, DK_A, DV_A), F32)],
        compiler_params=_cparams(("parallel", "arbitrary")),
    )(q, k, v, gam, gam_rows, q, k, v, gam, gam_rows)


def _attn_kernel(q_ref, k_ref, v_ref, o_ref, m_ref, l_ref, acc_ref):
    kv = pl.program_id(3)

    @pl.when(kv == 0)
    def _():
        m_ref[...] = jnp.full_like(m_ref, -jnp.inf)
        l_ref[...] = jnp.zeros_like(l_ref)
        acc_ref[...] = jnp.zeros_like(acc_ref)

    q = q_ref[0, 0]
    nk = k_ref.shape[2] // _ATTN_KSUB
    nt = _ATTN_KSUB // LANES
    ksub = lambda j: slice(j * _ATTN_KSUB, (j + 1) * _ATTN_KSUB)
    m = m_ref[...]
    l = l_ref[...]
    acc = acc_ref[...]
    s = _mm_nt(q, k_ref[0, 0, ksub(0), :])
    for j in range(nk):
        s_next = _mm_nt(q, k_ref[0, 0, ksub(j + 1), :]) if j + 1 < nk else None
        tiles = [s[:, t * LANES:(t + 1) * LANES] for t in range(nt)]
        m_new = jnp.maximum(m, jnp.max(functools.reduce(jnp.maximum, tiles), axis=-1, keepdims=True))
        alpha = jnp.exp2(m - m_new)
        ps = [jnp.exp2(t - m_new) for t in tiles]
        l = alpha * l + functools.reduce(lambda a, b: a + b, ps)
        p = jnp.concatenate([t.astype(MXU_DTYPE) for t in ps], axis=-1)
        acc = alpha * acc + jnp.dot(p, v_ref[0, 0, ksub(j), :], preferred_element_type=F32)
        m = m_new
        s = s_next
    m_ref[...] = m
    l_ref[...] = l
    acc_ref[...] = acc

    @pl.when(kv == pl.num_programs(3) - 1)
    def _():
        o_ref[0] = (acc / jnp.sum(l, axis=-1, keepdims=True)).astype(o_ref.dtype)


_ATTN_KSUB = 512


def _attention(qc, kc, vc, B, S):
    tq = min(1024, S)
    tk = min(4096, S)
    qc = qc.reshape(H_C, B, S, QK_HEAD_DIM)
    kc = kc.reshape(H_C, B, S, QK_HEAD_DIM)
    vc = vc.reshape(H_C, B, S, DV_C)
    return pl.pallas_call(
        _attn_kernel, name="attention", grid=(B, H_C, S // tq, S // tk),
        in_specs=[pl.BlockSpec((1, 1, tq, QK_HEAD_DIM), lambda b, h, i, j: (h, b, i, 0)),
                  pl.BlockSpec((1, 1, tk, QK_HEAD_DIM), lambda b, h, i, j: (h, b, j, 0)),
                  pl.BlockSpec((1, 1, tk, DV_C), lambda b, h, i, j: (h, b, j, 0))],
        out_specs=pl.BlockSpec((1, tq, DV_C), lambda b, h, i, j: (b, i, h)),
        out_shape=jax.ShapeDtypeStruct((B, S, H_C * DV_C), MXU_DTYPE),
        scratch_shapes=[pltpu.VMEM((tq, LANES), F32), pltpu.VMEM((tq, LANES), F32), pltpu.VMEM((tq, DV_C), F32)],
        compiler_params=_cparams(("parallel", "parallel", "parallel", "arbitrary")),
    )(qc, kc, vc)


def _merge_kernel(x_ref, of_ref, ob_ref, sz_ref, ub_ref, vb_ref, oc_ref, gate_ref, onorm_ref, sgw_ref, sgb_ref,
                  wbr_ref, wout_ref, h_ref):
    tm = x_ref.shape[0]
    o = of_ref[...] + ob_ref[...]
    sz = sz_ref[...]
    out_a = jnp.concatenate(
        [_rms(o[:, h * DV_A:(h + 1) * DV_A], onorm_ref[...]) * sz[:, h * DV_A:(h + 1) * DV_A] for h in range(H_A)],
        axis=-1)
    gw = SG_WIDTH // SG_GROUPS
    chunks = []
    for ci in range(tm // SG_CHUNK):
        rows = slice(ci * SG_CHUNK, (ci + 1) * SG_CHUNK)
        mixed = jnp.concatenate(
            [jnp.dot(sgw_ref[g], vb_ref[rows, g * gw:(g + 1) * gw], preferred_element_type=F32) + sgb_ref[:, g:g + 1]
             for g in range(SG_GROUPS)], axis=-1)
        chunks.append(ub_ref[rows, :] * mixed)
    out_b = jnp.concatenate(chunks, axis=0)
    gate = gate_ref[...]
    merged = (gate[:, :D_MODEL] * _mm(out_a, wbr_ref[0])
              + gate[:, D_MODEL:2 * D_MODEL] * _mm(out_b, wbr_ref[1])
              + gate[:, 2 * D_MODEL:] * jnp.dot(oc_ref[...], wbr_ref[2], preferred_element_type=F32))
    h_ref[...] = x_ref[...] + _mm(merged, wout_ref[...])


def _merge(x, o_f, o_b, sz, ub, vb, oc, gate, lw):
    T = x.shape[0]
    tm = min(256, T)
    row = lambda i: (i, 0)
    w512 = pl.BlockSpec((tm, 512), row)
    return pl.pallas_call(
        _merge_kernel, name="merge", grid=(T // tm,),
        in_specs=[pl.BlockSpec((tm, D_MODEL), row), w512, w512, w512, w512, w512, w512,
                  pl.BlockSpec((tm, N_BRANCH * D_MODEL), row),
                  pl.BlockSpec((1, DV_A), lambda i: (0, 0)),
                  _vmem_full(), _vmem_full(), _vmem_full(), _vmem_full()],
        out_specs=pl.BlockSpec((tm, D_MODEL), row),
        out_shape=jax.ShapeDtypeStruct((T, D_MODEL), F32),
        compiler_params=_cparams(("parallel",)),
    )(x, o_f, o_b, sz, ub, vb, oc, gate, lw["o_norm"], lw["sg_w"], lw["sg_bT"], lw["w_branch"], lw["w_out"])


_NEG_INF = float("-inf")
_CAND_B_LIMIT = tuple(PEER_TOPK // (a + 1) for a in range(PEER_TOPK))


GATE_DTYPE = jnp.bfloat16


def _pack_gate_pair(lo, hi):
    lo_bits = lax.bitcast_convert_type(lo.astype(GATE_DTYPE).astype(F32), jnp.uint32)
    hi_bits = lax.bitcast_convert_type(hi.astype(GATE_DTYPE).astype(F32), jnp.uint32)
    return (lo_bits >> 16) | (hi_bits & jnp.uint32(0xFFFF0000))


def _top16_rows(s, top_ref, exact):
    n, tt = s.shape
    rid = lax.broadcasted_iota(jnp.int32, (n, tt), 0)
    work = s
    pos = jnp.full((n, tt), float(PEER_TOPK), F32)
    for r in range(PEER_TOPK):
        m = jnp.max(work, axis=0, keepdims=True)
        sel = work == m
        if exact:
            sel = rid == jnp.min(jnp.where(sel, rid, n), axis=0, keepdims=True)
        pos = jnp.where(sel, float(r), pos)
        work = jnp.where(sel, _NEG_INF, work)
        top_ref[r:r + 1, :] = m
    return pos


def _peer_select_head(h, s0, s1, exact, pos1_ref, bm_ref, nbrow_ref, arow_ref, t0_ref, t1_ref):
    tt = s0.shape[1]
    sub = lax.broadcasted_iota(jnp.int32, (8, tt), 0)
    pos0 = _top16_rows(s0, t0_ref, exact)
    pos1 = _top16_rows(s1, t1_ref, exact)
    t0 = t0_ref[...]
    t1 = t1_ref[...]
    t00 = t0[0:1, :]
    t10 = t1[0:1, :]
    c00 = t00 + t10
    vals = [t00 + t1[0:8, :], t00 + t1[8:16, :]]
    cidx = [sub, sub + 8]
    for a in range(1, 8):
        vals.append(jnp.where(sub < _CAND_B_LIMIT[a], t0[a:a + 1, :] + t1[0:8, :], _NEG_INF))
        cidx.append(sub + a * PEER_TOPK)
    vals.append(t0[8:16, :] + t10)
    cidx.append((sub + 8) * PEER_TOPK)
    orig = list(vals)
    big = PEER_TOPK * PEER_TOPK
    for _ in range(PEER_TOPK):
        m = functools.reduce(jnp.maximum, vals)
        m = jnp.max(m, axis=0, keepdims=True)
        if exact:
            cand = functools.reduce(jnp.minimum, [jnp.where(v == m, ci, big) for v, ci in zip(vals, cidx)])
            idx = jnp.min(cand, axis=0, keepdims=True)
            vals = [jnp.where(ci == idx, _NEG_INF, v) for v, ci in zip(vals, cidx)]
        else:
            vals = [jnp.where(v == m, _NEG_INF, v) for v in vals]
    sel = [jnp.where(v != o, 1.0, 0.0) for v, o in zip(vals, orig)]
    z = functools.reduce(
        lambda x, y: x + y,
        [jnp.sum(jnp.where(s > 0.0, jnp.exp(jnp.where(s > 0.0, o - c00, 0.0)), 0.0), axis=0, keepdims=True)
         for s, o in zip(sel, orig)])
    nb_lo = [jnp.sum(sel[0], axis=0, keepdims=True) + jnp.sum(sel[1], axis=0, keepdims=True)]
    for a in range(1, 8):
        nb_lo.append(jnp.sum(sel[a + 1], axis=0, keepdims=True))
    nb_hi = sel[9]
    nbrow = jnp.zeros((N_KEYS, tt), F32)
    for a in range(PEER_TOPK):
        nba = nb_lo[a] if a < 8 else nb_hi[a - 8:a - 7, :]
        nbrow = jnp.where(pos0 == float(a), nba, nbrow)
    inv_z = 1.0 / z
    in0 = pos0 < float(PEER_TOPK)
    in1 = pos1 < float(PEER_TOPK)
    arow = jnp.where(in0, jnp.exp(jnp.where(in0, s0 - t00, 0.0)), 0.0) * inv_z
    bm = jnp.where(in1, jnp.exp(jnp.where(in1, s1 - t10, 0.0)), 0.0)
    nbrow_ref[h] = _pack_gate_pair(nbrow, nbrow)
    arow_ref[h] = _pack_gate_pair(arow, arow)
    half = N_KEYS // 2
    pos1_ref[h] = _pack_gate_pair(pos1[:half], pos1[half:])
    bm_ref[h] = _pack_gate_pair(bm[:half], bm[half:])
    k = float(PEER_TOPK)
    count = lambda flags: jnp.sum(jnp.where(flags, 1.0, 0.0), axis=0, keepdims=True)
    chosen = functools.reduce(lambda x, y: x + y, nb_lo) + jnp.sum(nb_hi, axis=0, keepdims=True)
    return jnp.abs(count(in0) - k) + jnp.abs(count(in1) - k) + jnp.abs(chosen - k)


def _peer_topk_kernel(h_ref, gain_ref, wq_ref, keys_ref, xn_ref, pos1_ref, bm_ref, nbrow_ref, arow_ref,
                      q_ref, t0_ref, t1_ref):
    xn = _rms(h_ref[...], gain_ref[...]).astype(MXU_DTYPE)
    xn_ref[...] = xn
    q_ref[...] = jnp.dot(xn, wq_ref[...], preferred_element_type=F32).astype(MXU_DTYPE)
    outs = (pos1_ref, bm_ref, nbrow_ref, arow_ref, t0_ref, t1_ref)

    def head(h, carry):
        base = pl.multiple_of(h * PEER_QDIM, PEER_QDIM)
        s0 = _mm_nt(keys_ref[h, 0], q_ref[:, pl.ds(base, N_KEYS)])
        s1 = _mm_nt(keys_ref[h, 1], q_ref[:, pl.ds(base + N_KEYS, N_KEYS)])
        off = _peer_select_head(h, s0, s1, False, *outs)

        @pl.when(jnp.max(off) > 0.0)
        def _():
            _peer_select_head(h, s0, s1, True, *outs)

        return carry

    lax.fori_loop(0, PEER_HEADS, head, 0)


def _peer_topk(h, lw):
    T = h.shape[0]
    tt = min(256, T)
    half = N_KEYS // 2
    rows_meta = jax.ShapeDtypeStruct((PEER_HEADS, N_KEYS, T), jnp.uint32)
    pair_meta = jax.ShapeDtypeStruct((PEER_HEADS, half, T), jnp.uint32)
    rows_spec = pl.BlockSpec((PEER_HEADS, N_KEYS, tt), lambda i: (0, 0, i))
    pair_spec = pl.BlockSpec((PEER_HEADS, half, tt), lambda i: (0, 0, i))
    return pl.pallas_call(
        _peer_topk_kernel, name="peer_topk", grid=(T // tt,),
        in_specs=[pl.BlockSpec((tt, D_MODEL), lambda i: (i, 0)), pl.BlockSpec((1, D_MODEL), lambda i: (0, 0)),
                  _vmem_full(), _vmem_full()],
        out_specs=(pl.BlockSpec((tt, D_MODEL), lambda i: (i, 0)), pair_spec, pair_spec, rows_spec, rows_spec),
        out_shape=(jax.ShapeDtypeStruct((T, D_MODEL), MXU_DTYPE), pair_meta, pair_meta, rows_meta, rows_meta),
        scratch_shapes=[pltpu.VMEM((tt, PEER_HEADS * PEER_QDIM), MXU_DTYPE), pltpu.VMEM((PEER_TOPK, tt), F32),
                        pltpu.VMEM((PEER_TOPK, tt), F32)],
        compiler_params=_cparams(("parallel",)),
    )(h, lw["ffn_norm"], lw["peer_wq"], lw["peer_keys"])


_PEER_IB = 16


def _peer_dense_kernel(h_ref, xn_ref, u_ref, vt_ref, pos1_ref, bm_ref, nbrow_ref, arow_ref, o_ref,
                       acc_ref, hid_ref, gh_ref):
    j = pl.program_id(1)
    tt = xn_ref.shape[0]
    half = N_KEYS // 2
    unpack = lambda words: pltpu.bitcast(words, GATE_DTYPE)

    @pl.when(j == 0)
    def _():
        acc_ref[...] = jnp.zeros_like(acc_ref)

    hid_ref[...] = _mm_nt(u_ref[...], xn_ref[...])
    row0 = pl.ds(pl.multiple_of(j * _PEER_IB, _PEER_IB), _PEER_IB)
    for il in range(_PEER_IB):
        rows = slice(il * N_KEYS, (il + 1) * N_KEYS)
        for st in range(tt // LANES):
            lanes = slice(st * LANES, (st + 1) * LANES)
            g = None
            for h in range(PEER_HEADS):
                nb = unpack(jnp.broadcast_to(nbrow_ref[h, row0, lanes][il:il + 1, :], (half, LANES)))
                ar = unpack(jnp.broadcast_to(arow_ref[h, row0, lanes][il:il + 1, :], (half, LANES)))
                term = jnp.where(unpack(pos1_ref[h, :, lanes]) < nb, unpack(bm_ref[h, :, lanes]), 0) * ar
                g = term if g is None else g + term
            gh_ref[rows, lanes] = g * _gelu(hid_ref[rows, lanes].astype(GATE_DTYPE))
    acc_ref[...] += jnp.dot(vt_ref[...], gh_ref[...].astype(vt_ref.dtype), preferred_element_type=F32)

    @pl.when(j == pl.num_programs(1) - 1)
    def _():
        o_ref[...] = h_ref[...] + acc_ref[...].T


def _peer_dense(h, xn, pos1, bm, nbrow, arow, lw):
    T = h.shape[0]
    tt = min(512, T)
    eb = _PEER_IB * N_KEYS
    rows_spec = pl.BlockSpec((PEER_HEADS, N_KEYS, tt), lambda i, j: (0, 0, i))
    pair_spec = pl.BlockSpec((PEER_HEADS, N_KEYS // 2, tt), lambda i, j: (0, 0, i))
    return pl.pallas_call(
        _peer_dense_kernel, name="peer_dense", grid=(T // tt, N_KEYS // _PEER_IB),
        in_specs=[pl.BlockSpec((tt, D_MODEL), lambda i, j: (i, 0)), pl.BlockSpec((tt, D_MODEL), lambda i, j: (i, 0)),
                  pl.BlockSpec((eb, D_MODEL), lambda i, j: (j, 0)), pl.BlockSpec((D_MODEL, eb), lambda i, j: (0, j)),
                  pair_spec, pair_spec, rows_spec, rows_spec],
        out_specs=pl.BlockSpec((tt, D_MODEL), lambda i, j: (i, 0)),
        out_shape=jax.ShapeDtypeStruct((T, D_MODEL), F32),
        scratch_shapes=[pltpu.VMEM((D_MODEL, tt), F32), pltpu.VMEM((eb, tt), F32), pltpu.VMEM((eb, tt), GATE_DTYPE)],
        compiler_params=_cparams(("parallel", "arbitrary")),
    )(h, xn, lw["peer_u"], lw["peer_vT"], pos1, bm, nbrow, arow)


def _pad_cols(w, width):
    return jnp.pad(w, ((0, 0), (0, width - w.shape[1])))


def _layer_weights(l, mix_norm, w_in, conv_w, a_log, dt_bias, o_norm, sg_norm, sg_w, sg_b, q_a_norm, w_uq, kv_a_norm,
                   w_ukv, q_nope_norm, q_rope_norm, k_nope_norm, k_rope_norm, w_branch, w_out, ffn_norm, peer_wq,
                   peer_keys, peer_u, peer_v):
    split = [QKV_A_COLS, 2 * H_A, 2 * H_A, H_A * DV_A, SG_WIDTH, SG_WIDTH, Q_LORA, KV_LORA, ROPE_DIM,
             N_BRANCH * D_MODEL]
    offs = [0]
    for s in split:
        offs.append(offs[-1] + s)
    wl = w_in[l]
    piece = lambda i: wl[:, offs[i]:offs[i + 1]]
    w_cat = jnp.concatenate([
        piece(0), _pad_cols(jnp.concatenate([piece(1), piece(2)], axis=1), LANES), piece(3), piece(4), piece(5),
        piece(6), piece(7), _pad_cols(piece(8), LANES), piece(9)], axis=1).astype(MXU_DTYPE)
    wq3 = w_uq[l].reshape(Q_LORA, H_C, QK_HEAD_DIM)
    w_uq_r = jnp.concatenate([wq3[:, :, :NOPE_DIM].reshape(Q_LORA, H_C * NOPE_DIM),
                              wq3[:, :, NOPE_DIM:].reshape(Q_LORA, H_C * ROPE_DIM)], axis=1).astype(MXU_DTYPE)
    row = lambda v: v.reshape(1, -1).astype(F32)
    return {
        "mix_norm": row(mix_norm[l]), "w_in": w_cat, "conv_w": conv_w[l].astype(F32),
        "a_log": _pad_cols(row(a_log[l]), LANES), "dt_bias": _pad_cols(row(dt_bias[l]), LANES),
        "o_norm": row(o_norm[l]), "sg_norm": row(sg_norm[l]), "sg_w": sg_w[l].astype(MXU_DTYPE),
        "sg_bT": sg_b[l].T.astype(F32), "q_a_norm": row(q_a_norm[l]), "w_uq": w_uq_r,
        "kv_a_norm": row(kv_a_norm[l]), "w_ukv": w_ukv[l].astype(MXU_DTYPE), "q_nope_norm": row(q_nope_norm[l]),
        "q_rope_norm": row(q_rope_norm[l]), "k_nope_norm": row(k_nope_norm[l]), "k_rope_norm": row(k_rope_norm[l]),
        "w_branch": w_branch[l].astype(MXU_DTYPE), "w_out": w_out[l].astype(MXU_DTYPE), "ffn_norm": row(ffn_norm[l]),
        "peer_wq": peer_wq[l].astype(MXU_DTYPE), "peer_keys": peer_keys[l].astype(MXU_DTYPE),
        "peer_u": _pair_order(peer_u[l].astype(MXU_DTYPE)), "peer_vT": _pair_order(peer_v[l].astype(MXU_DTYPE)).T,
    }


def _pair_order(table):
    half = N_KEYS // 2
    return table.reshape(N_KEYS, 2, half, D_MODEL).swapaxes(1, 2).reshape(N_KEYS * N_KEYS, D_MODEL)


def _rope_tables(S):
    inv_freq = ROPE_THETA ** (-jnp.arange(0, ROPE_DIM, 2, dtype=F32) / ROPE_DIM)
    ang = jnp.arange(S, dtype=F32)[:, None] * inv_freq[None, :]
    cos, sin = jnp.cos(ang), jnp.sin(ang)
    return jnp.concatenate([cos, cos], axis=-1), jnp.concatenate([-sin, sin], axis=-1)


def _mixer(x, B, S, tables, lw):
    qkv, gam, sz, ub, vb, qc, kc, vc, gate = _inproj(x, S, tables[0], tables[1], lw)
    q, k, v = _conv_prep(qkv.reshape(B, S, QKV_A_COLS), lw["conv_w"])
    nch = S // DELTA_CHUNK
    gam_rows = jnp.swapaxes(gam[:, :2 * H_A].reshape(B, nch, DELTA_CHUNK, 2 * H_A), 2, 3)
    o_f, o_b = _delta_rule(q, k, v, gam.reshape(B, S, LANES), gam_rows)
    oc = _attention(qc, kc, vc, B, S)
    T = B * S
    return _merge(x, o_f.reshape(T, -1), o_b.reshape(T, -1), sz, ub, vb, oc.reshape(T, -1), gate, lw)


def _peer(h, lw):
    xn, pos1, bm, nbrow, arow = _peer_topk(h, lw)
    return _peer_dense(h, xn, pos1, bm, nbrow, arow, lw)


def kernel(x_prompt, x_sample, mix_norm, w_in, conv_w, a_log, dt_bias, o_norm, sg_norm, sg_w, sg_b, q_a_norm, w_uq,
           kv_a_norm, w_ukv, q_nope_norm, q_rope_norm, k_nope_norm, k_rope_norm, w_branch, w_out, ffn_norm, peer_wq,
           peer_keys, peer_u, peer_v):
    params = (mix_norm, w_in, conv_w, a_log, dt_bias, o_norm, sg_norm, sg_w, sg_b, q_a_norm, w_uq, kv_a_norm, w_ukv,
              q_nope_norm, q_rope_norm, k_nope_norm, k_rope_norm, w_branch, w_out, ffn_norm, peer_wq, peer_keys,
              peer_u, peer_v)
    depth = w_in.shape[0]
    trunks = [x_prompt, x_sample]
    shapes = [t.shape for t in trunks]
    acts = [t.reshape(-1, D_MODEL) for t in trunks]
    tables = [_rope_tables(s[1]) for s in shapes]
    sizes = [a.shape[0] for a in acts]
    for l in range(depth):
        lw = _layer_weights(l, *params)
        mixed = [_mixer(a, s[0], s[1], t, lw) for a, s, t in zip(acts, shapes, tables)]
        y = _peer(jnp.concatenate(mixed, axis=0), lw)
        acts = [y[:sizes[0]], y[sizes[0]:]]
    return tuple(a.reshape(s) for a, s in zip(acts, shapes))
```

```python
import functools
import math

import jax
import jax.numpy as jnp
from jax import lax
from jax.experimental import pallas as pl
from jax.experimental.pallas import tpu as pltpu

D_MODEL = 1024
NORM_EPS = 1e-6
H_A = 4
DK_A = 128
DV_A = 128
CONV_W = 5
DELTA_CHUNK = 64
QKV_A_COLS = 3 * H_A * DK_A
SG_GROUPS = 4
SG_CHUNK = 128
SG_WIDTH = 512
H_C = 4
Q_LORA = 384
KV_LORA = 256
NOPE_DIM = 128
ROPE_DIM = 64
DV_C = 128
QK_HEAD_DIM = NOPE_DIM + ROPE_DIM
ROPE_THETA = 10000.0
SM_SCALE = QK_HEAD_DIM ** -0.5
_LOG2E = math.log2(math.e)
PEER_HEADS = 8
N_KEYS = 128
PEER_QDIM = 256
PEER_TOPK = 16
N_BRANCH = 3

MXU_DTYPE = jnp.bfloat16
LANES = 128
VMEM_LIMIT = 56 * 1024 * 1024

_IN_GROUPS = (("qkv", QKV_A_COLS), ("ab", LANES), ("z", 512), ("u", SG_WIDTH), ("v", SG_WIDTH),
              ("cq", Q_LORA), ("ckv", KV_LORA), ("kr", LANES), ("gate", N_BRANCH * D_MODEL))
_IN_OFF = {}
_off = 0
for _name, _w in _IN_GROUPS:
    _IN_OFF[_name] = (_off, _off + _w)
    _off += _w
IN_COLS_PAD = _off

F32 = jnp.float32


def _mm(a, b):
    return jnp.dot(a.astype(MXU_DTYPE), b.astype(MXU_DTYPE), preferred_element_type=F32)


def _mm_nt(a, b):
    return lax.dot_general(a.astype(MXU_DTYPE), b.astype(MXU_DTYPE), (((1,), (1,)), ((), ())),
                           preferred_element_type=F32)


def _mm_tn(a, b):
    return lax.dot_general(a.astype(MXU_DTYPE), b.astype(MXU_DTYPE), (((0,), (0,)), ((), ())),
                           preferred_element_type=F32)


def _rms(x, gain):
    return x * lax.rsqrt(jnp.mean(x * x, axis=-1, keepdims=True) + NORM_EPS) * gain


def _sigmoid(x):
    return 1.0 / (1.0 + jnp.exp(-x))


def _silu(x):
    return x * _sigmoid(x)


def _gelu(x):
    return x * (0.5 * (1.0 + jnp.tanh(math.sqrt(2.0 / math.pi) * (x + 0.044715 * (x * x * x)))))


def _softplus(x):
    return jnp.maximum(x, 0.0) + jnp.log1p(jnp.exp(-jnp.abs(x)))


def _cparams(sem):
    return pltpu.CompilerParams(dimension_semantics=sem, vmem_limit_bytes=VMEM_LIMIT)


def _vmem_full():
    return pl.BlockSpec(memory_space=pltpu.VMEM)


def _inproj_kernel(x_ref, cos_ref, sin_ref, mixg_ref, win_ref, alog_ref, dtb_ref, sgn_ref, qan_ref, wuq_ref,
                   kvan_ref, wukv_ref, qnn_ref, qrn_ref, knn_ref, krn_ref,
                   qkv_ref, gam_ref, sz_ref, ub_ref, vb_ref, qc_ref, kc_ref, vc_ref, gate_ref):
    tm = x_ref.shape[0]
    xn = _rms(x_ref[...], mixg_ref[...]).astype(MXU_DTYPE)

    def proj(name):
        a, b = _IN_OFF[name]
        return jnp.dot(xn, win_ref[:, a:b], preferred_element_type=F32)

    qkv_ref[...] = proj("qkv")
    sz_ref[...] = _silu(proj("z"))
    ub_ref[...] = _gelu(proj("u"))
    vb_ref[...] = _rms(_gelu(proj("v")), sgn_ref[...]).astype(vb_ref.dtype)
    gate_ref[...] = _sigmoid(proj("gate"))

    ab = proj("ab")
    col = lax.broadcasted_iota(jnp.int32, ab.shape, 1)
    la = jnp.where(col < 2 * H_A, -jnp.exp(alog_ref[...]) * _softplus(ab + dtb_ref[...]), 0.0)
    beta = _sigmoid(ab)
    r = lax.broadcasted_iota(jnp.int32, (tm, tm), 0)
    c = lax.broadcasted_iota(jnp.int32, (tm, tm), 1)
    same = (r // DELTA_CHUNK) == (c // DELTA_CHUNK)
    tri_f = jnp.where(same & (r >= c), 1.0, 0.0).astype(MXU_DTYPE)
    tri_b = jnp.where(same & (r <= c), 1.0, 0.0).astype(MXU_DTYPE)
    p0 = la.astype(MXU_DTYPE)
    r1 = la - p0.astype(F32)
    p1 = r1.astype(MXU_DTYPE)
    p2 = (r1 - p1.astype(F32)).astype(MXU_DTYPE)
    gf = (jnp.dot(tri_f, p0, preferred_element_type=F32) + jnp.dot(tri_f, p1, preferred_element_type=F32)
          + jnp.dot(tri_f, p2, preferred_element_type=F32))
    gb = (jnp.dot(tri_b, p0, preferred_element_type=F32) + jnp.dot(tri_b, p1, preferred_element_type=F32)
          + jnp.dot(tri_b, p2, preferred_element_type=F32))
    gam = jnp.where(col < H_A, gf, gb)
    gam_ref[...] = jnp.where(col < 2 * H_A, gam, jnp.where(col < 4 * H_A, beta, 0.0))

    cos2 = cos_ref[...]
    sin2 = sin_ref[...]

    def rope(t):
        swapped = jnp.concatenate([t[:, ROPE_DIM // 2:], t[:, :ROPE_DIM // 2]], axis=-1)
        return t * cos2 + swapped * sin2

    q = _mm(_rms(proj("cq"), qan_ref[...]), wuq_ref[...])
    kv = _mm(_rms(proj("ckv"), kvan_ref[...]), wukv_ref[...])
    kr = rope(_rms(proj("kr")[:, :ROPE_DIM], krn_ref[...]))
    for h in range(H_C):
        qn = _rms(q[:, h * NOPE_DIM:(h + 1) * NOPE_DIM], qnn_ref[...])
        o = H_C * NOPE_DIM + h * ROPE_DIM
        qr = rope(_rms(q[:, o:o + ROPE_DIM], qrn_ref[...]))
        qc_ref[h] = (jnp.concatenate([qn, qr], axis=-1) * (SM_SCALE * _LOG2E)).astype(qc_ref.dtype)
        kn = _rms(kv[:, h * 256:h * 256 + NOPE_DIM], knn_ref[...])
        kc_ref[h] = jnp.concatenate([kn, kr], axis=-1).astype(kc_ref.dtype)
        vc_ref[h] = kv[:, h * 256 + NOPE_DIM:(h + 1) * 256].astype(vc_ref.dtype)


def _inproj(x, seq, cos2, sin2, lw):
    T = x.shape[0]
    tm = min(256, seq)
    nseq = seq // tm
    row = lambda i: (i, 0)
    full2 = lambda shape: pl.BlockSpec(shape, lambda i: (0, 0))
    in_specs = [
        pl.BlockSpec((tm, D_MODEL), row),
        pl.BlockSpec((tm, ROPE_DIM), lambda i: (i % nseq, 0)),
        pl.BlockSpec((tm, ROPE_DIM), lambda i: (i % nseq, 0)),
        full2((1, D_MODEL)),
        _vmem_full(),
        full2((1, LANES)), full2((1, LANES)), full2((1, SG_WIDTH)), full2((1, Q_LORA)),
        _vmem_full(),
        full2((1, KV_LORA)),
        _vmem_full(),
        full2((1, NOPE_DIM)), full2((1, ROPE_DIM)), full2((1, NOPE_DIM)), full2((1, ROPE_DIM)),
    ]
    head3 = lambda w: pl.BlockSpec((H_C, tm, w), lambda i: (0, i, 0))
    out_shape = (
        jax.ShapeDtypeStruct((T, QKV_A_COLS), F32),
        jax.ShapeDtypeStruct((T, LANES), F32),
        jax.ShapeDtypeStruct((T, 512), F32),
        jax.ShapeDtypeStruct((T, SG_WIDTH), F32),
        jax.ShapeDtypeStruct((T, SG_WIDTH), MXU_DTYPE),
        jax.ShapeDtypeStruct((H_C, T, QK_HEAD_DIM), MXU_DTYPE),
        jax.ShapeDtypeStruct((H_C, T, QK_HEAD_DIM), MXU_DTYPE),
        jax.ShapeDtypeStruct((H_C, T, DV_C), MXU_DTYPE),
        jax.ShapeDtypeStruct((T, N_BRANCH * D_MODEL), F32),
    )
    out_specs = (
        pl.BlockSpec((tm, QKV_A_COLS), row), pl.BlockSpec((tm, LANES), row), pl.BlockSpec((tm, 512), row),
        pl.BlockSpec((tm, SG_WIDTH), row), pl.BlockSpec((tm, SG_WIDTH), row),
        head3(QK_HEAD_DIM), head3(QK_HEAD_DIM), head3(DV_C),
        pl.BlockSpec((tm, N_BRANCH * D_MODEL), row),
    )
    return pl.pallas_call(
        _inproj_kernel, name="inproj", grid=(T // tm,), in_specs=in_specs, out_specs=out_specs, out_shape=out_shape,
        compiler_params=_cparams(("parallel",)),
    )(x, cos2, sin2, lw["mix_norm"], lw["w_in"], lw["a_log"], lw["dt_bias"], lw["sg_norm"], lw["q_a_norm"],
      lw["w_uq"], lw["kv_a_norm"], lw["w_ukv"], lw["q_nope_norm"], lw["q_rope_norm"], lw["k_nope_norm"],
      lw["k_rope_norm"])


_HALO = 8


def _conv_kernel(prev_ref, x_ref, next_ref, w_ref, q_ref, k_ref, v_ref, ext_ref):
    i = pl.program_id(1)
    n = pl.num_programs(1)
    ts = x_ref.shape[1]
    ext_ref[0:_HALO, :] = jnp.where(i > 0, prev_ref[0], 0.0)
    ext_ref[_HALO:_HALO + ts, :] = x_ref[0]
    ext_ref[_HALO + ts:, :] = jnp.where(i < n - 1, next_ref[0], 0.0)
    acc = None
    for j in range(CONV_W):
        start = _HALO - CONV_W // 2 + j
        term = ext_ref[start:start + ts, :] * w_ref[j:j + 1, :]
        acc = term if acc is None else acc + term
    y = _silu(acc)
    for h in range(H_A):
        qh = y[:, h * DK_A:(h + 1) * DK_A]
        kh = y[:, (H_A + h) * DK_A:(H_A + h + 1) * DK_A]
        q_ref[0, :, h * DK_A:(h + 1) * DK_A] = qh * (
            lax.rsqrt(jnp.sum(qh * qh, axis=-1, keepdims=True) + NORM_EPS) * (DK_A ** -0.5))
        k_ref[0, :, h * DK_A:(h + 1) * DK_A] = kh * lax.rsqrt(jnp.sum(kh * kh, axis=-1, keepdims=True) + NORM_EPS)
    v_ref[0] = y[:, 2 * H_A * DK_A:]


def _conv_prep(qkv, conv_w):
    B, S, C = qkv.shape
    ts = min(512, S)
    nb = S // ts
    hb = ts // _HALO
    nh = S // _HALO
    out = jax.ShapeDtypeStruct((B, S, H_A * DK_A), F32)
    blk = pl.BlockSpec((1, ts, H_A * DK_A), lambda b, i: (b, i, 0))
    return pl.pallas_call(
        _conv_kernel, name="conv_prep", grid=(B, nb),
        in_specs=[
            pl.BlockSpec((1, _HALO, C), lambda b, i: (b, jnp.maximum(i * hb - 1, 0), 0)),
            pl.BlockSpec((1, ts, C), lambda b, i: (b, i, 0)),
            pl.BlockSpec((1, _HALO, C), lambda b, i: (b, jnp.minimum((i + 1) * hb, nh - 1), 0)),
            pl.BlockSpec((CONV_W, C), lambda b, i: (0, 0)),
        ],
        out_specs=(blk, blk, blk), out_shape=(out, out, out),
        scratch_shapes=[pltpu.VMEM((ts + 2 * _HALO, C), F32)],
        compiler_params=_cparams(("parallel", "parallel")),
    )(qkv, qkv, qkv, conv_w)


_DELTA_CHUNKS_PER_ITER = 4


def _delta_units(q, k, v, gam_c, gam_r, b_c, state, backward):
    C = DELTA_CHUNK
    U = range(len(q))
    r = lax.broadcasted_iota(jnp.int32, (C, C), 0)
    c = lax.broadcasted_iota(jnp.int32, (C, C), 1)
    incl = [r <= c if backward[u] else r >= c for u in U]
    strict = [r < c if backward[u] else r > c for u in U]
    last = [0 if backward[u] else C - 1 for u in U]
    decay = [jnp.where(incl[u], jnp.exp(jnp.where(incl[u], gam_c[u] - gam_r[u], 0.0)), 0.0) for u in U]
    kb = [k[u] * b_c[u] for u in U]
    a = [jnp.where(strict[u], _mm_nt(kb[u], k[u]) * decay[u], 0.0) for u in U]
    qk = [_mm_nt(q[u], k[u]) * decay[u] for u in U]
    eg = [jnp.exp(gam_c[u]) for u in U]
    y = [jnp.concatenate([v[u] * b_c[u], kb[u] * eg[u]], axis=-1) for u in U]
    y = [y[u] - _mm(a[u], y[u]) for u in U]
    p = a
    for _ in range(5):
        p = [_mm(p[u], p[u]) for u in U]
        y = [y[u] + _mm(p[u], y[u]) for u in U]
    g_last = [gam_c[u][last[u]:last[u] + 1, :] for u in U]
    k_dec = [k[u] * jnp.exp(g_last[u] - gam_c[u]) for u in U]
    q_dec = [q[u] * eg[u] for u in U]
    outs = []
    for g0 in range(0, len(q), len(state)):
        G = range(len(state))
        v_new = [y[g0 + i][:, :DV_A] - _mm(y[g0 + i][:, DV_A:], state[i]) for i in G]
        o_state = [_mm(q_dec[g0 + i], state[i]) for i in G]
        outs += [o_state[i] + _mm(qk[g0 + i], v_new[i]) for i in G]
        state = [state[i] * jnp.exp(g_last[g0 + i]) + _mm_tn(k_dec[g0 + i], v_new[i]) for i in G]
    return outs, state


def _delta_kernel(qf_ref, kf_ref, vf_ref, gf_ref, rf_ref, qb_ref, kb_ref, vb_ref, gb_ref, rb_ref,
                  of_ref, ob_ref, state_ref):
    C = DELTA_CHUNK
    nc = qf_ref.shape[1] // C

    @pl.when(pl.program_id(1) == 0)
    def _():
        state_ref[...] = jnp.zeros_like(state_ref)

    per_iter = _DELTA_CHUNKS_PER_ITER if nc % _DELTA_CHUNKS_PER_ITER == 0 else 1

    def body(it, carry):
        args = [[] for _ in range(7)]
        dests = []
        for sub in range(per_iter):
            ci = it * per_iter + sub
            for d, (q_ref, k_ref, v_ref, g_ref, r_ref, o_ref) in enumerate(
                    ((qf_ref, kf_ref, vf_ref, gf_ref, rf_ref, of_ref),
                     (qb_ref, kb_ref, vb_ref, gb_ref, rb_ref, ob_ref))):
                cc = ci if d == 0 else nc - 1 - ci
                rows = pl.ds(pl.multiple_of(cc * C, C), C)
                g = g_ref[0, rows, :]
                grow = r_ref[0, cc]
                for h in range(H_A):
                    lanes = slice(h * DK_A, (h + 1) * DK_A)
                    j = d * H_A + h
                    unit = (q_ref[0, rows, lanes], k_ref[0, rows, lanes], v_ref[0, rows, lanes], g[:, j:j + 1],
                            grow[j:j + 1, :], g[:, 2 * H_A + j:2 * H_A + j + 1], d == 1)
                    for lst, val in zip(args, unit):
                        lst.append(val)
                    dests.append((o_ref, rows, lanes))
        states = [state_ref[j] for j in range(2 * H_A)]
        outs, states = _delta_units(*args[:6], states, args[6])
        for (o_ref, rows, lanes), o in zip(dests, outs):
            o_ref[0, rows, lanes] = o
        for j, s_new in enumerate(states):
            state_ref[j] = s_new
        return carry

    lax.fori_loop(0, nc // per_iter, body, 0)


def _delta_rule(q, k, v, gam, gam_rows):
    B, S, W = q.shape
    ts = min(512, S)
    nb = S // ts
    ncb = ts // DELTA_CHUNK
    fwd = lambda b, i: (b, i, 0)
    bwd = lambda b, i: (b, nb - 1 - i, 0)
    fwd4 = lambda b, i: (b, i, 0, 0)
    bwd4 = lambda b, i: (b, nb - 1 - i, 0, 0)

    def specs(m3, m4):
        return [pl.BlockSpec((1, ts, W), m3), pl.BlockSpec((1, ts, W), m3), pl.BlockSpec((1, ts, W), m3),
                pl.BlockSpec((1, ts, LANES), m3), pl.BlockSpec((1, ncb, 2 * H_A, DELTA_CHUNK), m4)]

    out = jax.ShapeDtypeStruct((B, S, W), F32)
    return pl.pallas_call(
        _delta_kernel, name="delta_rule", grid=(B, nb), in_specs=specs(fwd, fwd4) + specs(bwd, bwd4),
        out_specs=(pl.BlockSpec((1, ts, W), fwd), pl.BlockSpec((1, ts, W), bwd)), out_shape=(out, out),
        scratch_shapes=[pltpu.VMEM((2 * H_A, DK_A, DV_A), F32)],
        compiler_params=_cparams(("parallel", "arbitrary")),
    )(q, k, v, gam, gam_rows, q, k, v, gam, gam_rows)


def _attn_kernel(q_ref, k_ref, v_ref, o_ref, m_ref, l_ref, acc_ref):
    kv = pl.program_id(3)

    @pl.when(kv == 0)
    def _():
        m_ref[...] = jnp.full_like(m_ref, -jnp.inf)
        l_ref[...] = jnp.zeros_like(l_ref)
        acc_ref[...] = jnp.zeros_like(acc_ref)

    q = q_ref[0, 0]
    nk = k_ref.shape[2] // _ATTN_KSUB
    nt = _ATTN_KSUB // LANES
    ksub = lambda j: slice(j * _ATTN_KSUB, (j + 1) * _ATTN_KSUB)
    m = m_ref[...]
    l = l_ref[...]
    acc = acc_ref[...]
    s = _mm_nt(q, k_ref[0, 0, ksub(0), :])
    for j in range(nk):
        s_next = _mm_nt(q, k_ref[0, 0, ksub(j + 1), :]) if j + 1 < nk else None
        tiles = [s[:, t * LANES:(t + 1) * LANES] for t in range(nt)]
        m_new = jnp.maximum(m, jnp.max(functools.reduce(jnp.maximum, tiles), axis=-1, keepdims=True))
        alpha = jnp.exp2(m - m_new)
        ps = [jnp.exp2(t - m_new) for t in tiles]
        l = alpha * l + functools.reduce(lambda a, b: a + b, ps)
        p = jnp.concatenate([t.astype(MXU_DTYPE) for t in ps], axis=-1)
        acc = alpha * acc + jnp.dot(p, v_ref[0, 0, ksub(j), :], preferred_element_type=F32)
        m = m_new
        s = s_next
    m_ref[...] = m
    l_ref[...] = l
    acc_ref[...] = acc

    @pl.when(kv == pl.num_programs(3) - 1)
    def _():
        o_ref[0] = (acc / jnp.sum(l, axis=-1, keepdims=True)).astype(o_ref.dtype)


_ATTN_KSUB = 512


def _attention(qc, kc, vc, B, S):
    tq = min(2048, S)
    tk = min(4096, S)
    qc = qc.reshape(H_C, B, S, QK_HEAD_DIM)
    kc = kc.reshape(H_C, B, S, QK_HEAD_DIM)
    vc = vc.reshape(H_C, B, S, DV_C)
    return pl.pallas_call(
        _attn_kernel, name="attention", grid=(B, H_C, S // tq, S // tk),
        in_specs=[pl.BlockSpec((1, 1, tq, QK_HEAD_DIM), lambda b, h, i, j: (h, b, i, 0)),
                  pl.BlockSpec((1, 1, tk, QK_HEAD_DIM), lambda b, h, i, j: (h, b, j, 0)),
                  pl.BlockSpec((1, 1, tk, DV_C), lambda b, h, i, j: (h, b, j, 0))],
        out_specs=pl.BlockSpec((1, tq, DV_C), lambda b, h, i, j: (b, i, h)),
        out_shape=jax.ShapeDtypeStruct((B, S, H_C * DV_C), MXU_DTYPE),
        scratch_shapes=[pltpu.VMEM((tq, LANES), F32), pltpu.VMEM((tq, LANES), F32), pltpu.VMEM((tq, DV_C), F32)],
        compiler_params=_cparams(("parallel", "parallel", "parallel", "arbitrary")),
    )(qc, kc, vc)


def _merge_kernel(x_ref, of_ref, ob_ref, sz_ref, ub_ref, vb_ref, oc_ref, gate_ref, onorm_ref, sgw_ref, sgb_ref,
                  wbr_ref, wout_ref, h_ref):
    tm = x_ref.shape[0]
    o = of_ref[...] + ob_ref[...]
    sz = sz_ref[...]
    out_a = jnp.concatenate(
        [_rms(o[:, h * DV_A:(h + 1) * DV_A], onorm_ref[...]) * sz[:, h * DV_A:(h + 1) * DV_A] for h in range(H_A)],
        axis=-1)
    gw = SG_WIDTH // SG_GROUPS
    chunks = []
    for ci in range(tm // SG_CHUNK):
        rows = slice(ci * SG_CHUNK, (ci + 1) * SG_CHUNK)
        mixed = jnp.concatenate(
            [jnp.dot(sgw_ref[g], vb_ref[rows, g * gw:(g + 1) * gw], preferred_element_type=F32) + sgb_ref[:, g:g + 1]
             for g in range(SG_GROUPS)], axis=-1)
        chunks.append(ub_ref[rows, :] * mixed)
    out_b = jnp.concatenate(chunks, axis=0)
    gate = gate_ref[...]
    merged = (gate[:, :D_MODEL] * _mm(out_a, wbr_ref[0])
              + gate[:, D_MODEL:2 * D_MODEL] * _mm(out_b, wbr_ref[1])
              + gate[:, 2 * D_MODEL:] * jnp.dot(oc_ref[...], wbr_ref[2], preferred_element_type=F32))
    h_ref[...] = x_ref[...] + _mm(merged, wout_ref[...])


def _merge(x, o_f, o_b, sz, ub, vb, oc, gate, lw):
    T = x.shape[0]
    tm = min(512, T)
    row = lambda i: (i, 0)
    w512 = pl.BlockSpec((tm, 512), row)
    return pl.pallas_call(
        _merge_kernel, name="merge", grid=(T // tm,),
        in_specs=[pl.BlockSpec((tm, D_MODEL), row), w512, w512, w512, w512, w512, w512,
                  pl.BlockSpec((tm, N_BRANCH * D_MODEL), row),
                  pl.BlockSpec((1, DV_A), lambda i: (0, 0)),
                  _vmem_full(), _vmem_full(), _vmem_full(), _vmem_full()],
        out_specs=pl.BlockSpec((tm, D_MODEL), row),
        out_shape=jax.ShapeDtypeStruct((T, D_MODEL), F32),
        compiler_params=_cparams(("parallel",)),
    )(x, o_f, o_b, sz, ub, vb, oc, gate, lw["o_norm"], lw["sg_w"], lw["sg_bT"], lw["w_branch"], lw["w_out"])


_NEG_INF = float("-inf")
_CAND_B_LIMIT = tuple(PEER_TOPK // (a + 1) for a in range(PEER_TOPK))


GATE_DTYPE = jnp.bfloat16


def _pack_gate_pair(lo, hi):
    lo_bits = lax.bitcast_convert_type(lo.astype(GATE_DTYPE).astype(F32), jnp.uint32)
    hi_bits = lax.bitcast_convert_type(hi.astype(GATE_DTYPE).astype(F32), jnp.uint32)
    return (lo_bits >> 16) | (hi_bits & jnp.uint32(0xFFFF0000))


def _top16_rows(s, top_ref, exact):
    n, tt = s.shape
    rid = lax.broadcasted_iota(jnp.int32, (n, tt), 0)
    work = s
    pos = jnp.full((n, tt), float(PEER_TOPK), F32)
    for r in range(PEER_TOPK):
        m = jnp.max(work, axis=0, keepdims=True)
        sel = work == m
        if exact:
            sel = rid == jnp.min(jnp.where(sel, rid, n), axis=0, keepdims=True)
        pos = jnp.where(sel, float(r), pos)
        work = jnp.where(sel, _NEG_INF, work)
        top_ref[r:r + 1, :] = m
    return pos


def _peer_select_head(h, s0, s1, exact, pos1_ref, bm_ref, nbrow_ref, arow_ref, t0_ref, t1_ref):
    tt = s0.shape[1]
    sub = lax.broadcasted_iota(jnp.int32, (8, tt), 0)
    pos0 = _top16_rows(s0, t0_ref, exact)
    pos1 = _top16_rows(s1, t1_ref, exact)
    t0 = t0_ref[...]
    t1 = t1_ref[...]
    t00 = t0[0:1, :]
    t10 = t1[0:1, :]
    c00 = t00 + t10
    vals = [t00 + t1[0:8, :], t00 + t1[8:16, :]]
    cidx = [sub, sub + 8]
    for a in range(1, 8):
        vals.append(jnp.where(sub < _CAND_B_LIMIT[a], t0[a:a + 1, :] + t1[0:8, :], _NEG_INF))
        cidx.append(sub + a * PEER_TOPK)
    vals.append(t0[8:16, :] + t10)
    cidx.append((sub + 8) * PEER_TOPK)
    orig = list(vals)
    big = PEER_TOPK * PEER_TOPK
    for _ in range(PEER_TOPK):
        m = functools.reduce(jnp.maximum, vals)
        m = jnp.max(m, axis=0, keepdims=True)
        if exact:
            cand = functools.reduce(jnp.minimum, [jnp.where(v == m, ci, big) for v, ci in zip(vals, cidx)])
            idx = jnp.min(cand, axis=0, keepdims=True)
            vals = [jnp.where(ci == idx, _NEG_INF, v) for v, ci in zip(vals, cidx)]
        else:
            vals = [jnp.where(v == m, _NEG_INF, v) for v in vals]
    sel = [jnp.where(v != o, 1.0, 0.0) for v, o in zip(vals, orig)]
    z = functools.reduce(
        lambda x, y: x + y,
        [jnp.sum(jnp.where(s > 0.0, jnp.exp(jnp.where(s > 0.0, o - c00, 0.0)), 0.0), axis=0, keepdims=True)
         for s, o in zip(sel, orig)])
    nb_lo = [jnp.sum(sel[0], axis=0, keepdims=True) + jnp.sum(sel[1], axis=0, keepdims=True)]
    for a in range(1, 8):
        nb_lo.append(jnp.sum(sel[a + 1], axis=0, keepdims=True))
    nb_hi = sel[9]
    nbrow = jnp.zeros((N_KEYS, tt), F32)
    for a in range(PEER_TOPK):
        nba = nb_lo[a] if a < 8 else nb_hi[a - 8:a - 7, :]
        nbrow = jnp.where(pos0 == float(a), nba, nbrow)
    inv_z = 1.0 / z
    in0 = pos0 < float(PEER_TOPK)
    in1 = pos1 < float(PEER_TOPK)
    arow = jnp.where(in0, jnp.exp(jnp.where(in0, s0 - t00, 0.0)), 0.0) * inv_z
    bm = jnp.where(in1, jnp.exp(jnp.where(in1, s1 - t10, 0.0)), 0.0)
    nbrow_ref[h] = _pack_gate_pair(nbrow, nbrow)
    arow_ref[h] = _pack_gate_pair(arow, arow)
    half = N_KEYS // 2
    pos1_ref[h] = _pack_gate_pair(pos1[:half], pos1[half:])
    bm_ref[h] = _pack_gate_pair(bm[:half], bm[half:])
    k = float(PEER_TOPK)
    count = lambda flags: jnp.sum(jnp.where(flags, 1.0, 0.0), axis=0, keepdims=True)
    chosen = functools.reduce(lambda x, y: x + y, nb_lo) + jnp.sum(nb_hi, axis=0, keepdims=True)
    return jnp.abs(count(in0) - k) + jnp.abs(count(in1) - k) + jnp.abs(chosen - k)


def _peer_topk_kernel(h_ref, gain_ref, wq_ref, keys_ref, xn_ref, pos1_ref, bm_ref, nbrow_ref, arow_ref,
                      q_ref, t0_ref, t1_ref):
    xn = _rms(h_ref[...], gain_ref[...]).astype(MXU_DTYPE)
    xn_ref[...] = xn
    q_ref[...] = jnp.dot(xn, wq_ref[...], preferred_element_type=F32).astype(MXU_DTYPE)
    outs = (pos1_ref, bm_ref, nbrow_ref, arow_ref, t0_ref, t1_ref)

    def head(h, carry):
        base = pl.multiple_of(h * PEER_QDIM, PEER_QDIM)
        s0 = _mm_nt(keys_ref[h, 0], q_ref[:, pl.ds(base, N_KEYS)])
        s1 = _mm_nt(keys_ref[h, 1], q_ref[:, pl.ds(base + N_KEYS, N_KEYS)])
        off = _peer_select_head(h, s0, s1, False, *outs)

        @pl.when(jnp.max(off) > 0.0)
        def _():
            _peer_select_head(h, s0, s1, True, *outs)

        return carry

    lax.fori_loop(0, PEER_HEADS, head, 0)


def _peer_topk(h, lw):
    T = h.shape[0]
    tt = min(512, T)
    half = N_KEYS // 2
    rows_meta = jax.ShapeDtypeStruct((PEER_HEADS, N_KEYS, T), jnp.uint32)
    pair_meta = jax.ShapeDtypeStruct((PEER_HEADS, half, T), jnp.uint32)
    rows_spec = pl.BlockSpec((PEER_HEADS, N_KEYS, tt), lambda i: (0, 0, i))
    pair_spec = pl.BlockSpec((PEER_HEADS, half, tt), lambda i: (0, 0, i))
    return pl.pallas_call(
        _peer_topk_kernel, name="peer_topk", grid=(T // tt,),
        in_specs=[pl.BlockSpec((tt, D_MODEL), lambda i: (i, 0)), pl.BlockSpec((1, D_MODEL), lambda i: (0, 0)),
                  _vmem_full(), _vmem_full()],
        out_specs=(pl.BlockSpec((tt, D_MODEL), lambda i: (i, 0)), pair_spec, pair_spec, rows_spec, rows_spec),
        out_shape=(jax.ShapeDtypeStruct((T, D_MODEL), MXU_DTYPE), pair_meta, pair_meta, rows_meta, rows_meta),
        scratch_shapes=[pltpu.VMEM((tt, PEER_HEADS * PEER_QDIM), MXU_DTYPE), pltpu.VMEM((PEER_TOPK, tt), F32),
                        pltpu.VMEM((PEER_TOPK, tt), F32)],
        compiler_params=_cparams(("parallel",)),
    )(h, lw["ffn_norm"], lw["peer_wq"], lw["peer_keys"])


_PEER_IB = 8


def _peer_dense_kernel(h_ref, xn_ref, u_ref, vt_ref, pos1_ref, bm_ref, nbrow_ref, arow_ref, o_ref,
                       acc_ref, hid_ref, gh_ref):
    j = pl.program_id(1)
    tt = xn_ref.shape[0]
    half = N_KEYS // 2
    unpack = lambda words: pltpu.bitcast(words, GATE_DTYPE)

    @pl.when(j == 0)
    def _():
        acc_ref[...] = jnp.zeros_like(acc_ref)

    hid_ref[...] = _mm_nt(u_ref[...], xn_ref[...])
    row0 = pl.ds(pl.multiple_of(j * _PEER_IB, _PEER_IB), _PEER_IB)
    for il in range(_PEER_IB):
        rows = slice(il * N_KEYS, (il + 1) * N_KEYS)
        for st in range(tt // LANES):
            lanes = slice(st * LANES, (st + 1) * LANES)
            g = None
            for h in range(PEER_HEADS):
                nb = unpack(jnp.broadcast_to(nbrow_ref[h, row0, lanes][il:il + 1, :], (half, LANES)))
                ar = unpack(jnp.broadcast_to(arow_ref[h, row0, lanes][il:il + 1, :], (half, LANES)))
                term = jnp.where(unpack(pos1_ref[h, :, lanes]) < nb, unpack(bm_ref[h, :, lanes]), 0) * ar
                g = term if g is None else g + term
            gh_ref[rows, lanes] = g * _gelu(hid_ref[rows, lanes].astype(GATE_DTYPE))
    acc_ref[...] += jnp.dot(vt_ref[...], gh_ref[...].astype(vt_ref.dtype), preferred_element_type=F32)

    @pl.when(j == pl.num_programs(1) - 1)
    def _():
        o_ref[...] = h_ref[...] + acc_ref[...].T


def _peer_dense(h, xn, pos1, bm, nbrow, arow, lw):
    T = h.shape[0]
    tt = 768 if T % 768 == 0 else min(512, T)
    eb = _PEER_IB * N_KEYS
    rows_spec = pl.BlockSpec((PEER_HEADS, N_KEYS, tt), lambda i, j: (0, 0, i))
    pair_spec = pl.BlockSpec((PEER_HEADS, N_KEYS // 2, tt), lambda i, j: (0, 0, i))
    return pl.pallas_call(
        _peer_dense_kernel, name="peer_dense", grid=(T // tt, N_KEYS // _PEER_IB),
        in_specs=[pl.BlockSpec((tt, D_MODEL), lambda i, j: (i, 0)), pl.BlockSpec((tt, D_MODEL), lambda i, j: (i, 0)),
                  pl.BlockSpec((eb, D_MODEL), lambda i, j: (j, 0)), pl.BlockSpec((D_MODEL, eb), lambda i, j: (0, j)),
                  pair_spec, pair_spec, rows_spec, rows_spec],
        out_specs=pl.BlockSpec((tt, D_MODEL), lambda i, j: (i, 0)),
        out_shape=jax.ShapeDtypeStruct((T, D_MODEL), F32),
        scratch_shapes=[pltpu.VMEM((D_MODEL, tt), F32), pltpu.VMEM((eb, tt), F32), pltpu.VMEM((eb, tt), GATE_DTYPE)],
        compiler_params=_cparams(("parallel", "arbitrary")),
    )(h, xn, lw["peer_u"], lw["peer_vT"], pos1, bm, nbrow, arow)


def _pad_cols(w, width):
    return jnp.pad(w, ((0, 0), (0, width - w.shape[1])))


def _layer_weights(l, mix_norm, w_in, conv_w, a_log, dt_bias, o_norm, sg_norm, sg_w, sg_b, q_a_norm, w_uq, kv_a_norm,
                   w_ukv, q_nope_norm, q_rope_norm, k_nope_norm, k_rope_norm, w_branch, w_out, ffn_norm, peer_wq,
                   peer_keys, peer_u, peer_v):
    split = [QKV_A_COLS, 2 * H_A, 2 * H_A, H_A * DV_A, SG_WIDTH, SG_WIDTH, Q_LORA, KV_LORA, ROPE_DIM,
             N_BRANCH * D_MODEL]
    offs = [0]
    for s in split:
        offs.append(offs[-1] + s)
    wl = w_in[l]
    piece = lambda i: wl[:, offs[i]:offs[i + 1]]
    w_cat = jnp.concatenate([
        piece(0), _pad_cols(jnp.concatenate([piece(1), piece(2)], axis=1), LANES), piece(3), piece(4), piece(5),
        piece(6), piece(7), _pad_cols(piece(8), LANES), piece(9)], axis=1).astype(MXU_DTYPE)
    wq3 = w_uq[l].reshape(Q_LORA, H_C, QK_HEAD_DIM)
    w_uq_r = jnp.concatenate([wq3[:, :, :NOPE_DIM].reshape(Q_LORA, H_C * NOPE_DIM),
                              wq3[:, :, NOPE_DIM:].reshape(Q_LORA, H_C * ROPE_DIM)], axis=1).astype(MXU_DTYPE)
    row = lambda v: v.reshape(1, -1).astype(F32)
    return {
        "mix_norm": row(mix_norm[l]), "w_in": w_cat, "conv_w": conv_w[l].astype(F32),
        "a_log": _pad_cols(row(a_log[l]), LANES), "dt_bias": _pad_cols(row(dt_bias[l]), LANES),
        "o_norm": row(o_norm[l]), "sg_norm": row(sg_norm[l]), "sg_w": sg_w[l].astype(MXU_DTYPE),
        "sg_bT": sg_b[l].T.astype(F32), "q_a_norm": row(q_a_norm[l]), "w_uq": w_uq_r,
        "kv_a_norm": row(kv_a_norm[l]), "w_ukv": w_ukv[l].astype(MXU_DTYPE), "q_nope_norm": row(q_nope_norm[l]),
        "q_rope_norm": row(q_rope_norm[l]), "k_nope_norm": row(k_nope_norm[l]), "k_rope_norm": row(k_rope_norm[l]),
        "w_branch": w_branch[l].astype(MXU_DTYPE), "w_out": w_out[l].astype(MXU_DTYPE), "ffn_norm": row(ffn_norm[l]),
        "peer_wq": peer_wq[l].astype(MXU_DTYPE), "peer_keys": peer_keys[l].astype(MXU_DTYPE),
        "peer_u": _pair_order(peer_u[l].astype(MXU_DTYPE)), "peer_vT": _pair_order(peer_v[l].astype(MXU_DTYPE)).T,
    }


def _pair_order(table):
    half = N_KEYS // 2
    return table.reshape(N_KEYS, 2, half, D_MODEL).swapaxes(1, 2).reshape(N_KEYS * N_KEYS, D_MODEL)


def _rope_tables(S):
    inv_freq = ROPE_THETA ** (-jnp.arange(0, ROPE_DIM, 2, dtype=F32) / ROPE_DIM)
    ang = jnp.arange(S, dtype=F32)[:, None] * inv_freq[None, :]
    cos, sin = jnp.cos(ang), jnp.sin(ang)
    return jnp.concatenate([cos, cos], axis=-1), jnp.concatenate([-sin, sin], axis=-1)


def _mixer(x, B, S, tables, lw):
    qkv, gam, sz, ub, vb, qc, kc, vc, gate = _inproj(x, S, tables[0], tables[1], lw)
    q, k, v = _conv_prep(qkv.reshape(B, S, QKV_A_COLS), lw["conv_w"])
    nch = S // DELTA_CHUNK
    gam_rows = jnp.swapaxes(gam[:, :2 * H_A].reshape(B, nch, DELTA_CHUNK, 2 * H_A), 2, 3)
    o_f, o_b = _delta_rule(q, k, v, gam.reshape(B, S, LANES), gam_rows)
    oc = _attention(qc, kc, vc, B, S)
    T = B * S
    return _merge(x, o_f.reshape(T, -1), o_b.reshape(T, -1), sz, ub, vb, oc.reshape(T, -1), gate, lw)


def _peer(h, lw):
    xn, pos1, bm, nbrow, arow = _peer_topk(h, lw)
    return _peer_dense(h, xn, pos1, bm, nbrow, arow, lw)


def kernel(x_prompt, x_sample, mix_norm, w_in, conv_w, a_log, dt_bias, o_norm, sg_norm, sg_w, sg_b, q_a_norm, w_uq,
           kv_a_norm, w_ukv, q_nope_norm, q_rope_norm, k_nope_norm, k_rope_norm, w_branch, w_out, ffn_norm, peer_wq,
           peer_keys, peer_u, peer_v):
    params = (mix_norm, w_in, conv_w, a_log, dt_bias, o_norm, sg_norm, sg_w, sg_b, q_a_norm, w_uq, kv_a_norm, w_ukv,
              q_nope_norm, q_rope_norm, k_nope_norm, k_rope_norm, w_branch, w_out, ffn_norm, peer_wq, peer_keys,
              peer_u, peer_v)
    depth = w_in.shape[0]
    trunks = [x_prompt, x_sample]
    shapes = [t.shape for t in trunks]
    acts = [t.reshape(-1, D_MODEL) for t in trunks]
    tables = [_rope_tables(s[1]) for s in shapes]
    sizes = [a.shape[0] for a in acts]
    for l in range(depth):
        lw = _layer_weights(l, *params)
        mixed = [_mixer(a, s[0], s[1], t, lw) for a, s, t in zip(acts, shapes, tables)]
        y = _peer(jnp.concatenate(mixed, axis=0), lw)
        acts = [y[:sizes[0]], y[sizes[0]:]]
    return tuple(a.reshape(s) for a, s in zip(acts, shapes))
```
